```python
import math
import jax, jax.numpy as jnp
from jax import lax
import numpy as np

D_MODEL = 1024
BATCH = 4
SEQ = 8192
DEPTH = 1
DEC_BATCH = 8
DEC_SEQ = 2048
PAST_LEN = 128

MIX_WIDTH = D_MODEL
ATTN_WIDTH = MIX_WIDTH // 2
RET_WIDTH = MIX_WIDTH - ATTN_WIDTH
ATTN_HEAD_DIM = 64
N_ATTN_HEADS = ATTN_WIDTH // ATTN_HEAD_DIM
N_KV_HEADS = 2
GQA_GROUP = N_ATTN_HEADS // N_KV_HEADS
WINDOW = 128
ATTN_BLOCK = 128
ROT_DIM = ATTN_HEAD_DIM // 4
ROPE_THETA = 500000.0
RET_HEAD_DIM = 128
N_RET_HEADS = RET_WIDTH // RET_HEAD_DIM
RET_CHUNK = 128
RET_ROT_THETA = 10000.0
D_FF = 2816
EPS = 1e-6
NEG_BIG = -1e30
D_IN = (N_ATTN_HEADS * ATTN_HEAD_DIM + 2 * N_KV_HEADS * ATTN_HEAD_DIM
        + 4 * RET_WIDTH)

kernel_name = "hymba_swa_retention_macaron_encoder"


def _rmsnorm(x, g):
    xf = x.astype(jnp.float32)
    y = xf * lax.rsqrt(jnp.mean(xf * xf, axis=-1, keepdims=True) + EPS)
    return (y * g.astype(jnp.float32)).astype(x.dtype)


def _swiglu(x, w_gate, w_up, w_down):
    return (jax.nn.silu(x @ w_gate) * (x @ w_up)) @ w_down


def _rotate(x, cos, sin):
    half = x.shape[-1] // 2
    x1, x2 = x[..., :half], x[..., half:]
    return jnp.concatenate([x1 * cos - x2 * sin, x2 * cos + x1 * sin], axis=-1)


def _banded_sink_attention(q, k, v, sink):
    B, S = q.shape[0], q.shape[1]
    C = ATTN_BLOCK
    NB = S // C
    d = ATTN_HEAD_DIM
    qb = q.reshape(B, NB, C, N_KV_HEADS, GQA_GROUP, d) * (d ** -0.5)
    pad = ((0, 0), (C, C), (0, 0), (0, 0))
    kp = jnp.pad(k, pad).reshape(B, NB + 2, C, N_KV_HEADS, d)
    vp = jnp.pad(v, pad).reshape(B, NB + 2, C, N_KV_HEADS, d)
    kw = jnp.concatenate([kp[:, :-2], kp[:, 1:-1], kp[:, 2:]], axis=2)
    vw = jnp.concatenate([vp[:, :-2], vp[:, 1:-1], vp[:, 2:]], axis=2)
    scores = jnp.einsum('bnqhgd,bnkhd->bnhgqk', qb, kw,
                        preferred_element_type=jnp.float32)
    qpos = jnp.arange(NB)[:, None] * C + jnp.arange(C)[None, :]
    kpos = jnp.arange(NB)[:, None] * C - C + jnp.arange(3 * C)[None, :]
    rel = kpos[:, None, :] - qpos[:, :, None]
    valid = (jnp.abs(rel) <= WINDOW) & (kpos >= 0)[:, None, :] & (kpos < S)[:, None, :]
    scores = jnp.where(valid[None, :, None, None], scores, NEG_BIG)
    sink_l = sink.astype(jnp.float32).reshape(N_KV_HEADS, GQA_GROUP)[None, None, :, :, None, None]
    m = jnp.maximum(jnp.max(scores, axis=-1, keepdims=True), sink_l)
    p = jnp.exp(scores - m)
    p = p / (jnp.sum(p, axis=-1, keepdims=True) + jnp.exp(sink_l - m))
    out = jnp.einsum('bnhgqk,bnkhd->bnqhgd', p.astype(v.dtype), vw)
    return out.reshape(B, S, N_ATTN_HEADS * d)


def _retention_one_direction(q, k, v, log_decay):
    B, H, S, dk = q.shape
    dv = v.shape[-1]
    C = RET_CHUNK
    NC = S // C
    ld = log_decay.astype(jnp.float32)
    qc = q.astype(jnp.float32).reshape(B, H, NC, C, dk)
    kc = k.astype(jnp.float32).reshape(B, H, NC, C, dk)
    vc = v.astype(jnp.float32).reshape(B, H, NC, C, dv)
    idx = jnp.arange(C, dtype=jnp.float32)
    diff = idx[:, None] - idx[None, :]
    D = jnp.where(diff >= 0, jnp.exp(ld[:, None, None] * jnp.maximum(diff, 0.0)), 0.0)
    inner = jnp.einsum('bhnid,bhnjd->bhnij', qc, kc) * D[None, :, None]
    o_inner = jnp.einsum('bhnij,bhnje->bhnie', inner, vc)
    k_to_end = kc * jnp.exp(ld[:, None] * (C - 1.0 - idx))[None, :, None, :, None]
    kv_chunk = jnp.einsum('bhnjd,bhnje->nbhde', k_to_end, vc)
    chunk_decay = jnp.exp(ld * C)[None, :, None, None]

    def step(state, kv_n):
        return state * chunk_decay + kv_n, state

    _, states_prev = lax.scan(step, jnp.zeros((B, H, dk, dv), jnp.float32), kv_chunk)
    q_from_start = qc * jnp.exp(ld[:, None] * (idx + 1.0))[None, :, None, :, None]
    o_cross = jnp.einsum('bhnid,nbhde->bhnie', q_from_start, states_prev)
    return (o_inner + o_cross).reshape(B, H, S, dv)


def _layer(x, ffn1_norm, ffn1_w_gate, ffn1_w_up, ffn1_w_down, mix_norm, w_in,
           attn_sink, attn_out_norm, ret_log_decay_fwd, ret_log_decay_bwd, w_out,
           ffn2_norm, ffn2_w_gate, ffn2_w_up, ffn2_w_down):
    B, S, _ = x.shape
    h = x + 0.5 * _swiglu(_rmsnorm(x, ffn1_norm), ffn1_w_gate, ffn1_w_up, ffn1_w_down)

    u = _rmsnorm(h, mix_norm)
    proj = u @ w_in
    o1 = N_ATTN_HEADS * ATTN_HEAD_DIM
    o2 = o1 + N_KV_HEADS * ATTN_HEAD_DIM
    o3 = o2 + N_KV_HEADS * ATTN_HEAD_DIM
    o4 = o3 + RET_WIDTH
    o5 = o4 + RET_WIDTH
    o6 = o5 + RET_WIDTH
    aq, ak, av, rq, rk, rv, rg = jnp.split(proj, [o1, o2, o3, o4, o5, o6], axis=-1)

    pos = jnp.arange(S, dtype=jnp.float32)

    inv_a = ROPE_THETA ** (-jnp.arange(0, ROT_DIM, 2, dtype=jnp.float32) / ROT_DIM)
    ang_a = pos[:, None] * inv_a[None, :]
    cos_a = jnp.cos(ang_a)[None, :, None, :].astype(x.dtype)
    sin_a = jnp.sin(ang_a)[None, :, None, :].astype(x.dtype)
    aq = aq.reshape(B, S, N_ATTN_HEADS, ATTN_HEAD_DIM)
    ak = ak.reshape(B, S, N_KV_HEADS, ATTN_HEAD_DIM)
    av = av.reshape(B, S, N_KV_HEADS, ATTN_HEAD_DIM)
    aq = jnp.concatenate([_rotate(aq[..., :ROT_DIM], cos_a, sin_a), aq[..., ROT_DIM:]], axis=-1)
    ak = jnp.concatenate([_rotate(ak[..., :ROT_DIM], cos_a, sin_a), ak[..., ROT_DIM:]], axis=-1)
    attn_out = _rmsnorm(_banded_sink_attention(aq, ak, av, attn_sink), attn_out_norm)

    inv_r = RET_ROT_THETA ** (-jnp.linspace(0.0, 1.0, RET_HEAD_DIM // 2, dtype=jnp.float32))
    ang_r = pos[:, None] * inv_r[None, :]
    cos_r = jnp.cos(ang_r)[None, None].astype(x.dtype)
    sin_r = jnp.sin(ang_r)[None, None].astype(x.dtype)
    rq = rq.reshape(B, S, N_RET_HEADS, RET_HEAD_DIM).transpose(0, 2, 1, 3)
    rk = rk.reshape(B, S, N_RET_HEADS, RET_HEAD_DIM).transpose(0, 2, 1, 3)
    rv = rv.reshape(B, S, N_RET_HEADS, RET_HEAD_DIM).transpose(0, 2, 1, 3)
    rq = _rotate(rq, cos_r, sin_r)
    rk = _rotate(rk, cos_r, sin_r) * (RET_HEAD_DIM ** -0.5)
    ret_f = _retention_one_direction(rq, rk, rv, ret_log_decay_fwd)
    ret_b = jnp.flip(_retention_one_direction(jnp.flip(rq, 2), jnp.flip(rk, 2), jnp.flip(rv, 2),
                                              ret_log_decay_bwd), 2)
    ret = ret_f + ret_b
    mu = jnp.mean(ret, axis=-1, keepdims=True)
    var = jnp.mean(jnp.square(ret - mu), axis=-1, keepdims=True)
    ret = ((ret - mu) * lax.rsqrt(var + EPS)).astype(x.dtype)
    ret = ret.transpose(0, 2, 1, 3).reshape(B, S, RET_WIDTH)
    ret_out = jax.nn.silu(rg) * ret

    h = h + jnp.concatenate([attn_out, ret_out], axis=-1) @ w_out

    h = h + 0.5 * _swiglu(_rmsnorm(h, ffn2_norm), ffn2_w_gate, ffn2_w_up, ffn2_w_down)
    return h


def _trunk(x, ffn1_norm, ffn1_w_gate, ffn1_w_up, ffn1_w_down, mix_norm, w_in,
           attn_sink, attn_out_norm, ret_log_decay_fwd, ret_log_decay_bwd, w_out,
           ffn2_norm, ffn2_w_gate, ffn2_w_up, ffn2_w_down, final_norm):
    h = x
    for l in range(DEPTH):
        h = _layer(h, ffn1_norm[l], ffn1_w_gate[l], ffn1_w_up[l], ffn1_w_down[l],
                   mix_norm[l], w_in[l], attn_sink[l], attn_out_norm[l],
                   ret_log_decay_fwd[l], ret_log_decay_bwd[l], w_out[l],
                   ffn2_norm[l], ffn2_w_gate[l], ffn2_w_up[l], ffn2_w_down[l])
    return _rmsnorm(h, final_norm)


def setup_inputs(seed: int = 0) -> dict:
    key = jax.random.key(seed)
    ks = jax.random.split(key, 20)
    f32 = jnp.float32

    def w(k, shape, fan_in):
        return jax.random.normal(k, shape, f32) * (fan_in ** -0.5)

    def gain(k, shape):
        return 1.0 + 0.05 * jax.random.normal(k, shape, f32)

    base_decay = jnp.log1p(-jnp.exp2(-5.0 - jnp.arange(N_RET_HEADS, dtype=f32)))
    return {
        "x_prompt": jax.random.normal(ks[0], (BATCH, SEQ, D_MODEL), f32),
        "x_sample": jax.random.normal(ks[1], (DEC_BATCH, DEC_SEQ, D_MODEL), f32),
        "ffn1_norm": gain(ks[2], (DEPTH, D_MODEL)),
        "ffn1_w_gate": w(ks[3], (DEPTH, D_MODEL, D_FF), D_MODEL),
        "ffn1_w_up": w(ks[4], (DEPTH, D_MODEL, D_FF), D_MODEL),
        "ffn1_w_down": w(ks[5], (DEPTH, D_FF, D_MODEL), D_FF),
        "mix_norm": gain(ks[6], (DEPTH, D_MODEL)),
        "w_in": w(ks[7], (DEPTH, D_MODEL, D_IN), D_MODEL),
        "attn_sink": 0.5 * jax.random.normal(ks[8], (DEPTH, N_ATTN_HEADS), f32),
        "attn_out_norm": gain(ks[9], (DEPTH, ATTN_WIDTH)),
        "ret_log_decay_fwd": base_decay[None, :] * (1.0 + 0.05 * jax.random.normal(ks[10], (DEPTH, N_RET_HEADS), f32)),
        "ret_log_decay_bwd": base_decay[None, :] * (1.0 + 0.05 * jax.random.normal(ks[11], (DEPTH, N_RET_HEADS), f32)),
        "w_out": w(ks[12], (DEPTH, MIX_WIDTH, D_MODEL), MIX_WIDTH),
        "ffn2_norm": gain(ks[13], (DEPTH, D_MODEL)),
        "ffn2_w_gate": w(ks[14], (DEPTH, D_MODEL, D_FF), D_MODEL),
        "ffn2_w_up": w(ks[15], (DEPTH, D_MODEL, D_FF), D_MODEL),
        "ffn2_w_down": w(ks[16], (DEPTH, D_FF, D_MODEL), D_FF),
        "final_norm": gain(ks[17], (D_MODEL,)),
    }


def reference(x_prompt, x_sample, ffn1_norm, ffn1_w_gate, ffn1_w_up, ffn1_w_down,
              mix_norm, w_in, attn_sink, attn_out_norm, ret_log_decay_fwd,
              ret_log_decay_bwd, w_out, ffn2_norm, ffn2_w_gate, ffn2_w_up,
              ffn2_w_down, final_norm):
    y_prompt = _trunk(x_prompt, ffn1_norm, ffn1_w_gate, ffn1_w_up, ffn1_w_down, mix_norm,
                      w_in, attn_sink, attn_out_norm, ret_log_decay_fwd, ret_log_decay_bwd,
                      w_out, ffn2_norm, ffn2_w_gate, ffn2_w_up, ffn2_w_down, final_norm)
    y_sample = _trunk(x_sample, ffn1_norm, ffn1_w_gate, ffn1_w_up, ffn1_w_down, mix_norm,
                      w_in, attn_sink, attn_out_norm, ret_log_decay_fwd, ret_log_decay_bwd,
                      w_out, ffn2_norm, ffn2_w_gate, ffn2_w_up, ffn2_w_down, final_norm)
    return (y_prompt, y_sample)
```

```python
import functools

import jax
import jax.numpy as jnp
from jax import lax
from jax.experimental import pallas as pl
from jax.experimental.pallas import tpu as pltpu

D_MODEL = 1024
D_FF = 2816
N_ATTN_HEADS = 8
N_KV_HEADS = 2
ATTN_HEAD_DIM = 64
ATTN_WIDTH = N_ATTN_HEADS * ATTN_HEAD_DIM
WINDOW = 128
ATTN_BLOCK = 128
ROT_DIM = 16
ROPE_THETA = 500000.0
RET_HEAD_DIM = 128
N_RET_HEADS = 4
RET_WIDTH = N_RET_HEADS * RET_HEAD_DIM
RET_CHUNK = 128
RET_ROT_THETA = 10000.0
EPS = 1e-6
NEG_BIG = -1e30

LANES = 128
KV_DUP_WIDTH = 2 * LANES
FF_CHUNK = 256
TOKEN_TILE = 512
SEQ_TILE = 512
VMEM_LIMIT = 56 * 1024 * 1024

BF16 = jnp.bfloat16
F32 = jnp.float32


def _dot(a, b):
    return jnp.dot(a, b, preferred_element_type=F32)


def _dot_nt(a, b):
    return lax.dot_general(a, b, (((1,), (1,)), ((), ())), preferred_element_type=F32)


def _dot_tn(a, b):
    return lax.dot_general(a, b, (((0,), (0,)), ((), ())), preferred_element_type=F32)


def _rmsnorm(x, g):
    return x * lax.rsqrt(jnp.mean(x * x, axis=-1, keepdims=True) + EPS) * g


def _silu(x):
    return x * (1.0 / (1.0 + jnp.exp(-x)))


def _swiglu(xn, wg_ref, wu_ref, wd_ref, act_ref):
    for c in range(D_FF // FF_CHUNK):
        cols = slice(c * FF_CHUNK, (c + 1) * FF_CHUNK)
        g = _dot(xn, wg_ref[:, cols])
        u = _dot(xn, wu_ref[:, cols])
        act_ref[:, cols] = (_silu(g) * u).astype(BF16)
    return _dot(act_ref[...], wd_ref[...])


def _ffn_proj_kernel(x_ref, g1_ref, wg_ref, wu_ref, wd_ref, gm_ref, win_ref,
                     cosa_ref, sina_ref, sinb_ref, cosr_ref, sinr_ref,
                     h_ref, q_ref, k_ref, v_ref, rq_ref, rk_ref, rv_ref, rg_ref,
                     act_ref):
    x = x_ref[...]
    xn = _rmsnorm(x, g1_ref[...]).astype(BF16)
    h = x + 0.5 * _swiglu(xn, wg_ref, wu_ref, wd_ref, act_ref)
    h_ref[...] = h
    un = _rmsnorm(h, gm_ref[...]).astype(BF16)

    cosa, sina, sinb = cosa_ref[...], sina_ref[...], sinb_ref[...]
    cosr, sinr = cosr_ref[...], sinr_ref[...]

    def rope_a(xb):
        return (xb * cosa + pltpu.roll(xb, LANES - ROT_DIM // 2, 1) * sina
                + pltpu.roll(xb, ROT_DIM // 2, 1) * sinb)

    def rope_r(xb):
        return xb * cosr + pltpu.roll(xb, RET_HEAD_DIM // 2, 1) * sinr

    col = 0
    for c in range(ATTN_WIDTH // LANES):
        p = _dot(un, win_ref[:, col:col + LANES])
        q_ref[:, c * LANES:(c + 1) * LANES] = (rope_a(p) * (ATTN_HEAD_DIM ** -0.5)).astype(BF16)
        col += LANES
    for c in range(KV_DUP_WIDTH // LANES):
        p = _dot(un, win_ref[:, col:col + LANES])
        k_ref[:, c * LANES:(c + 1) * LANES] = rope_a(p).astype(BF16)
        col += LANES
    v_ref[...] = _dot(un, win_ref[:, col:col + KV_DUP_WIDTH]).astype(BF16)
    col += KV_DUP_WIDTH
    for c in range(N_RET_HEADS):
        p = _dot(un, win_ref[:, col:col + LANES])
        rq_ref[:, c * LANES:(c + 1) * LANES] = rope_r(p).astype(BF16)
        col += LANES
    for c in range(N_RET_HEADS):
        p = _dot(un, win_ref[:, col:col + LANES])
        rk_ref[:, c * LANES:(c + 1) * LANES] = (rope_r(p) * (RET_HEAD_DIM ** -0.5)).astype(BF16)
        col += LANES
    rv_ref[...] = _dot(un, win_ref[:, col:col + RET_WIDTH]).astype(BF16)
    col += RET_WIDTH
    rg_ref[...] = _dot(un, win_ref[:, col:col + RET_WIDTH]).astype(BF16)


def _const_spec(shape):
    return pl.BlockSpec(shape, lambda *_: (0,) * len(shape), pipeline_mode=pl.Buffered(1))


def _ffn_proj(x2d, seq_len, g1, wg, wu, wd, gm, win, tables):
    n = x2d.shape[0]
    tm = min(TOKEN_TILE, seq_len)
    tiles_per_seq = seq_len // tm
    row = lambda i: (i, 0)
    tab = lambda i: (i % tiles_per_seq, 0)
    win_cols = win.shape[1]
    out_widths = (ATTN_WIDTH, KV_DUP_WIDTH, KV_DUP_WIDTH, RET_WIDTH, RET_WIDTH, RET_WIDTH, RET_WIDTH)
    return pl.pallas_call(
        _ffn_proj_kernel,
        grid=(n // tm,),
        in_specs=[
            pl.BlockSpec((tm, D_MODEL), row),
            _const_spec((1, D_MODEL)),
            _const_spec((D_MODEL, D_FF)),
            _const_spec((D_MODEL, D_FF)),
            _const_spec((D_FF, D_MODEL)),
            _const_spec((1, D_MODEL)),
            _const_spec((D_MODEL, win_cols)),
        ] + [pl.BlockSpec((tm, LANES), tab)] * 5,
        out_specs=[pl.BlockSpec((tm, D_MODEL), row)] + [pl.BlockSpec((tm, w), row) for w in out_widths],
        out_shape=[jax.ShapeDtypeStruct((n, D_MODEL), F32)]
                  + [jax.ShapeDtypeStruct((n, w), BF16) for w in out_widths],
        scratch_shapes=[pltpu.VMEM((tm, D_FF), BF16)],
        compiler_params=pltpu.CompilerParams(dimension_semantics=("parallel",),
                                             vmem_limit_bytes=VMEM_LIMIT),
        name="ffn_proj",
    )(x2d, g1, wg, wu, wd, gm, win, *tables)


def _attn_kernel(sink_ref, q_ref, kp_ref, km_ref, kn_ref, vp_ref, vm_ref, vn_ref, gain_ref,
                 o_ref, kw_ref, vw_ref, acc_ref):
    t = pl.program_id(1)
    nt = pl.num_programs(1)
    ts = q_ref.shape[1]
    nblk = ts // ATTN_BLOCK
    seq_len = nt * ts

    kw_ref[0:ATTN_BLOCK, :] = kp_ref[0]
    kw_ref[ATTN_BLOCK:ATTN_BLOCK + ts, :] = km_ref[0]
    kw_ref[ATTN_BLOCK + ts:, :] = kn_ref[0]
    vw_ref[0:ATTN_BLOCK, :] = vp_ref[0]
    vw_ref[ATTN_BLOCK:ATTN_BLOCK + ts, :] = vm_ref[0]
    vw_ref[ATTN_BLOCK + ts:, :] = vn_ref[0]

    wk = 3 * ATTN_BLOCK
    qi = lax.broadcasted_iota(jnp.int32, (ATTN_BLOCK, wk), 0)
    kj = lax.broadcasted_iota(jnp.int32, (ATTN_BLOCK, wk), 1)
    band = (kj >= qi) & (kj <= qi + 2 * WINDOW)
    lane = lax.broadcasted_iota(jnp.int32, (wk, LANES), 1)
    lo_half = lane < ATTN_HEAD_DIM
    out_lane = lax.broadcasted_iota(jnp.int32, (ATTN_BLOCK, LANES), 1)
    out_lo = out_lane < ATTN_HEAD_DIM

    for j in range(nblk):
        kpos = (t * ts + (j - 1) * ATTN_BLOCK) + kj
        valid = band & (kpos >= 0) & (kpos < seq_len)
        rows = slice(j * ATTN_BLOCK, (j + 1) * ATTN_BLOCK)
        wrows = slice(j * ATTN_BLOCK, j * ATTN_BLOCK + wk)
        for g in range(N_KV_HEADS):
            kcols = slice(g * LANES, (g + 1) * LANES)
            kg = kw_ref[wrows, kcols]
            vg = vw_ref[wrows, kcols]
            k_sel = (jnp.where(lo_half, kg, jnp.zeros_like(kg)),
                     jnp.where(lo_half, jnp.zeros_like(kg), kg))
            for pair in range(2):
                qcols = slice((2 * g + pair) * LANES, (2 * g + pair + 1) * LANES)
                qp = q_ref[0, rows, qcols]
                res = []
                for half in range(2):
                    head = 4 * g + 2 * pair + half
                    s = _dot_nt(qp, k_sel[half])
                    s = jnp.where(valid, s, NEG_BIG)
                    sink = sink_ref[head]
                    m = jnp.maximum(jnp.max(s, axis=-1, keepdims=True), sink)
                    p = jnp.exp(s - m)
                    denom = jnp.sum(p, axis=-1, keepdims=True) + jnp.exp(sink - m)
                    res.append(_dot(p.astype(BF16), vg) / denom)
                acc_ref[rows, qcols] = jnp.where(out_lo, res[0], res[1])

    o_ref[0] = _rmsnorm(acc_ref[...], gain_ref[...]).astype(BF16)


def _attn(q, k, v, sink, gain):
    b, s, _ = q.shape
    ts = min(SEQ_TILE, s)
    nt = s // ts
    bps = ts // ATTN_BLOCK
    nb = s // ATTN_BLOCK
    main = lambda i, t: (i, t, 0)
    prev = lambda i, t: (i, jnp.maximum(t * bps - 1, 0), 0)
    nxt = lambda i, t: (i, jnp.minimum((t + 1) * bps, nb - 1), 0)
    kv_specs = [pl.BlockSpec((1, ATTN_BLOCK, KV_DUP_WIDTH), prev),
                pl.BlockSpec((1, ts, KV_DUP_WIDTH), main),
                pl.BlockSpec((1, ATTN_BLOCK, KV_DUP_WIDTH), nxt)]
    return pl.pallas_call(
        _attn_kernel,
        grid=(b, nt),
        in_specs=[pl.BlockSpec(memory_space=pltpu.SMEM),
                  pl.BlockSpec((1, ts, ATTN_WIDTH), main)] + kv_specs + kv_specs
                 + [pl.BlockSpec((1, ATTN_WIDTH), lambda i, t: (0, 0))],
        out_specs=pl.BlockSpec((1, ts, ATTN_WIDTH), main),
        out_shape=jax.ShapeDtypeStruct((b, s, ATTN_WIDTH), BF16),
        scratch_shapes=[pltpu.VMEM((ts + 2 * ATTN_BLOCK, KV_DUP_WIDTH), BF16),
                        pltpu.VMEM((ts + 2 * ATTN_BLOCK, KV_DUP_WIDTH), BF16),
                        pltpu.VMEM((ts, ATTN_WIDTH), F32)],
        compiler_params=pltpu.CompilerParams(dimension_semantics=("parallel", "parallel"),
                                             vmem_limit_bytes=VMEM_LIMIT),
        name="attn",
    )(sink, q, k, k, k, v, v, v, gain)


def _head_lane_vector(ld_ref, scale):
    lane = lax.broadcasted_iota(jnp.int32, (1, RET_WIDTH), 1)
    ld = jnp.zeros((1, RET_WIDTH), F32)
    for h in range(N_RET_HEADS):
        ld = jnp.where(lane >= h * RET_HEAD_DIM, ld_ref[h], ld)
    return jnp.exp(ld * scale)


def _ret_state_kernel(ldb_ref, rk_ref, rv_ref, sb_ref, state_ref):
    t = pl.program_id(1)
    nchunk = rk_ref.shape[1] // RET_CHUNK

    @pl.when(t == 0)
    def _():
        state_ref[...] = jnp.zeros_like(state_ref)

    row = lax.broadcasted_iota(jnp.int32, (RET_CHUNK, RET_HEAD_DIM), 0).astype(F32)
    chunk_decay = _head_lane_vector(ldb_ref, float(RET_CHUNK))
    for c in reversed(range(nchunk)):
        rows = slice(c * RET_CHUNK, (c + 1) * RET_CHUNK)
        sb_ref[0, c] = state_ref[...].astype(BF16)
        kvs = []
        for h in range(N_RET_HEADS):
            cols = slice(h * RET_HEAD_DIM, (h + 1) * RET_HEAD_DIM)
            kb = (rk_ref[0, rows, cols].astype(F32) * jnp.exp(ldb_ref[h] * row)).astype(BF16)
            kvs.append(_dot_tn(kb, rv_ref[0, rows, cols]))
        state_ref[...] = state_ref[...] * chunk_decay + jnp.concatenate(kvs, axis=1)


def _ret_state(rk, rv, ld_b):
    b, s, _ = rk.shape
    ts = min(SEQ_TILE, s)
    nt = s // ts
    cpt = ts // RET_CHUNK
    rev = lambda i, t: (i, nt - 1 - t, 0)
    return pl.pallas_call(
        _ret_state_kernel,
        grid=(b, nt),
        in_specs=[pl.BlockSpec(memory_space=pltpu.SMEM),
                  pl.BlockSpec((1, ts, RET_WIDTH), rev),
                  pl.BlockSpec((1, ts, RET_WIDTH), rev)],
        out_specs=pl.BlockSpec((1, cpt, RET_HEAD_DIM, RET_WIDTH), lambda i, t: (i, nt - 1 - t, 0, 0)),
        out_shape=jax.ShapeDtypeStruct((b, s // RET_CHUNK, RET_HEAD_DIM, RET_WIDTH), BF16),
        scratch_shapes=[pltpu.VMEM((RET_HEAD_DIM, RET_WIDTH), F32)],
        compiler_params=pltpu.CompilerParams(dimension_semantics=("parallel", "arbitrary"),
                                             vmem_limit_bytes=VMEM_LIMIT),
        name="ret_state",
    )(ld_b, rk, rv)


def _ret_out_kernel(ldf_ref, ldb_ref, rq_ref, rk_ref, rv_ref, rg_ref, sb_ref, o_ref,
                    state_ref, mask_ref):
    t = pl.program_id(1)
    nchunk = rq_ref.shape[1] // RET_CHUNK
    shape = (RET_CHUNK, RET_HEAD_DIM)
    row_i = lax.broadcasted_iota(jnp.int32, shape, 0)
    col_i = lax.broadcasted_iota(jnp.int32, shape, 1)
    row = row_i.astype(F32)

    @pl.when(t == 0)
    def _():
        state_ref[...] = jnp.zeros_like(state_ref)
        diff = (row_i - col_i).astype(F32)
        for h in range(N_RET_HEADS):
            d_fwd = jnp.where(diff >= 0, jnp.exp(ldf_ref[h] * jnp.maximum(diff, 0.0)), 0.0)
            d_bwd = jnp.where(diff <= 0, jnp.exp(ldb_ref[h] * jnp.maximum(-diff, 0.0)), 0.0)
            mask_ref[h] = d_fwd + d_bwd

    chunk_decay = _head_lane_vector(ldf_ref, float(RET_CHUNK))
    for c in range(nchunk):
        rows = slice(c * RET_CHUNK, (c + 1) * RET_CHUNK)
        kvs = []
        for h in range(N_RET_HEADS):
            cols = slice(h * RET_HEAD_DIM, (h + 1) * RET_HEAD_DIM)
            q = rq_ref[0, rows, cols]
            k = rk_ref[0, rows, cols]
            v = rv_ref[0, rows, cols]
            inner = (_dot_nt(q, k) * mask_ref[h]).astype(BF16)
            o = _dot(inner, v)
            o += jnp.exp(ldf_ref[h] * (row + 1.0)) * _dot(q, state_ref[:, cols].astype(BF16))
            o += jnp.exp(ldb_ref[h] * (float(RET_CHUNK) - row)) * _dot(q, sb_ref[0, c, :, cols])
            mu = jnp.mean(o, axis=-1, keepdims=True)
            var = jnp.mean(jnp.square(o - mu), axis=-1, keepdims=True)
            on = (o - mu) * lax.rsqrt(var + EPS)
            o_ref[0, rows, cols] = (_silu(rg_ref[0, rows, cols].astype(F32)) * on).astype(BF16)
            kf = (k.astype(F32) * jnp.exp(ldf_ref[h] * (float(RET_CHUNK) - 1.0 - row))).astype(BF16)
            kvs.append(_dot_tn(kf, v))
        state_ref[...] = state_ref[...] * chunk_decay + jnp.concatenate(kvs, axis=1)


def _ret_out(rq, rk, rv, rg, sb, ld_f, ld_b):
    b, s, _ = rq.shape
    ts = min(SEQ_TILE, s)
    nt = s // ts
    cpt = ts // RET_CHUNK
    main = lambda i, t: (i, t, 0)
    seq_spec = pl.BlockSpec((1, ts, RET_WIDTH), main)
    return pl.pallas_call(
        _ret_out_kernel,
        grid=(b, nt),
        in_specs=[pl.BlockSpec(memory_space=pltpu.SMEM), pl.BlockSpec(memory_space=pltpu.SMEM),
                  seq_spec, seq_spec, seq_spec, seq_spec,
                  pl.BlockSpec((1, cpt, RET_HEAD_DIM, RET_WIDTH), lambda i, t: (i, t, 0, 0))],
        out_specs=seq_spec,
        out_shape=jax.ShapeDtypeStruct((b, s, RET_WIDTH), BF16),
        scratch_shapes=[pltpu.VMEM((RET_HEAD_DIM, RET_WIDTH), F32),
                        pltpu.VMEM((N_RET_HEADS, RET_CHUNK, RET_HEAD_DIM), F32)],
        compiler_params=pltpu.CompilerParams(dimension_semantics=("parallel", "arbitrary"),
                                             vmem_limit_bytes=VMEM_LIMIT),
        name="ret_out",
    )(ld_f, ld_b, rq, rk, rv, rg, sb)


def _out_ffn_kernel(h_ref, a_ref, r_ref, woa_ref, wor_ref, g2_ref, wg_ref, wu_ref, wd_ref, gf_ref,
                    y_ref, act_ref):
    h = h_ref[...] + _dot(a_ref[...], woa_ref[...]) + _dot(r_ref[...], wor_ref[...])
    hn = _rmsnorm(h, g2_ref[...]).astype(BF16)
    h = h + 0.5 * _swiglu(hn, wg_ref, wu_ref, wd_ref, act_ref)
    y_ref[...] = _rmsnorm(h, gf_ref[...])


def _out_ffn(h, a, r, woa, wor, g2, wg, wu, wd, gf):
    n = h.shape[0]
    tm = min(TOKEN_TILE, n)
    row = lambda i: (i, 0)
    return pl.pallas_call(
        _out_ffn_kernel,
        grid=(n // tm,),
        in_specs=[
            pl.BlockSpec((tm, D_MODEL), row),
            pl.BlockSpec((tm, ATTN_WIDTH), row),
            pl.BlockSpec((tm, RET_WIDTH), row),
            _const_spec((ATTN_WIDTH, D_MODEL)),
            _const_spec((RET_WIDTH, D_MODEL)),
            _const_spec((1, D_MODEL)),
            _const_spec((D_MODEL, D_FF)),
            _const_spec((D_MODEL, D_FF)),
            _const_spec((D_FF, D_MODEL)),
            _const_spec((1, D_MODEL)),
        ],
        out_specs=pl.BlockSpec((tm, D_MODEL), row),
        out_shape=jax.ShapeDtypeStruct((n, D_MODEL), F32),
        scratch_shapes=[pltpu.VMEM((tm, D_FF), BF16)],
        compiler_params=pltpu.CompilerParams(dimension_semantics=("parallel",),
                                             vmem_limit_bytes=VMEM_LIMIT),
        name="out_ffn",
    )(h, a, r, woa, wor, g2, wg, wu, wd, gf)


def _rope_tables(seq_len):
    pos = jnp.arange(seq_len, dtype=F32)
    inv_a = ROPE_THETA ** (-jnp.arange(0, ROT_DIM, 2, dtype=F32) / ROT_DIM)
    ang_a = pos[:, None] * inv_a[None, :]
    cos8, sin8 = jnp.cos(ang_a), jnp.sin(ang_a)
    half = ROT_DIM // 2
    pad = ATTN_HEAD_DIM - ROT_DIM
    ones = jnp.ones((seq_len, pad), F32)
    zeros = jnp.zeros((seq_len, pad), F32)
    z8 = jnp.zeros((seq_len, half), F32)
    cos_a = jnp.concatenate([cos8, cos8, ones], axis=1)
    sin_a = jnp.concatenate([-sin8, z8, zeros], axis=1)
    sin_b = jnp.concatenate([z8, sin8, zeros], axis=1)
    rep = LANES // ATTN_HEAD_DIM
    cos_a, sin_a, sin_b = (jnp.tile(t, (1, rep)) for t in (cos_a, sin_a, sin_b))
    inv_r = RET_ROT_THETA ** (-jnp.linspace(0.0, 1.0, RET_HEAD_DIM // 2, dtype=F32))
    ang_r = pos[:, None] * inv_r[None, :]
    cos_r = jnp.concatenate([jnp.cos(ang_r)] * 2, axis=1)
    sin_r = jnp.concatenate([-jnp.sin(ang_r), jnp.sin(ang_r)], axis=1)
    return cos_a, sin_a, sin_b, cos_r, sin_r


def _dup_kv_columns(w):
    heads = [w[:, h * ATTN_HEAD_DIM:(h + 1) * ATTN_HEAD_DIM] for h in range(N_KV_HEADS)]
    return jnp.concatenate([heads[0], heads[0], heads[1], heads[1]], axis=1)


def _layer(x, p):
    b, s, _ = x.shape
    n = b * s
    tables = _rope_tables(s)
    h, q, k, v, rq, rk, rv, rg = _ffn_proj(
        x.reshape(n, D_MODEL), s, p["g1"], p["wg1"], p["wu1"], p["wd1"], p["gm"], p["win"], tables)
    seq = lambda a: a.reshape(b, s, a.shape[-1])
    attn = _attn(seq(q), seq(k), seq(v), p["sink"], p["ga"])
    sb = _ret_state(seq(rk), seq(rv), p["ldb"])
    ret = _ret_out(seq(rq), seq(rk), seq(rv), seq(rg), sb, p["ldf"], p["ldb"])
    y = _out_ffn(h, attn.reshape(n, ATTN_WIDTH), ret.reshape(n, RET_WIDTH),
                 p["woa"], p["wor"], p["g2"], p["wg2"], p["wu2"], p["wd2"], p["gf"])
    return y.reshape(b, s, D_MODEL)


def kernel(x_prompt, x_sample, ffn1_norm, ffn1_w_gate, ffn1_w_up, ffn1_w_down, mix_norm, w_in, attn_sink, attn_out_norm, ret_log_decay_fwd, ret_log_decay_bwd, w_out, ffn2_norm, ffn2_w_gate, ffn2_w_up, ffn2_w_down, final_norm):
    assert ffn1_norm.shape[0] == 1
    o1 = ATTN_WIDTH
    o2 = o1 + N_KV_HEADS * ATTN_HEAD_DIM
    o3 = o2 + N_KV_HEADS * ATTN_HEAD_DIM
    wi = w_in[0]
    win = jnp.concatenate([wi[:, :o1], _dup_kv_columns(wi[:, o1:o2]), _dup_kv_columns(wi[:, o2:o3]),
                           wi[:, o3:]], axis=1).astype(BF16)
    wo = w_out[0].astype(BF16)
    p = dict(
        g1=ffn1_norm[0][None, :], wg1=ffn1_w_gate[0].astype(BF16), wu1=ffn1_w_up[0].astype(BF16),
        wd1=ffn1_w_down[0].astype(BF16), gm=mix_norm[0][None, :], win=win,
        sink=attn_sink[0], ga=attn_out_norm[0][None, :],
        ldf=ret_log_decay_fwd[0], ldb=ret_log_decay_bwd[0],
        woa=wo[:ATTN_WIDTH], wor=wo[ATTN_WIDTH:],
        g2=ffn2_norm[0][None, :], wg2=ffn2_w_gate[0].astype(BF16), wu2=ffn2_w_up[0].astype(BF16),
        wd2=ffn2_w_down[0].astype(BF16), gf=final_norm[None, :],
    )
    return (_layer(x_prompt, p), _layer(x_sample, p))
```

```python
import functools

import jax
import jax.numpy as jnp
from jax import lax
from jax.experimental import pallas as pl
from jax.experimental.pallas import tpu as pltpu

D_MODEL = 1024
D_FF = 2816
N_ATTN_HEADS = 8
N_KV_HEADS = 2
ATTN_HEAD_DIM = 64
ATTN_WIDTH = N_ATTN_HEADS * ATTN_HEAD_DIM
WINDOW = 128
ATTN_BLOCK = 128
ROT_DIM = 16
ROPE_THETA = 500000.0
RET_HEAD_DIM = 128
N_RET_HEADS = 4
RET_WIDTH = N_RET_HEADS * RET_HEAD_DIM
RET_CHUNK = 128
RET_ROT_THETA = 10000.0
EPS = 1e-6
NEG_BIG = -1e30

LANES = 128
KV_DUP_WIDTH = 2 * LANES
FF_CHUNK = 256
TOKEN_TILE = 512
SEQ_TILE = 512
VMEM_LIMIT = 56 * 1024 * 1024

BF16 = jnp.bfloat16
F32 = jnp.float32


def _dot(a, b):
    return jnp.dot(a, b, preferred_element_type=F32)


def _dot_nt(a, b):
    return lax.dot_general(a, b, (((1,), (1,)), ((), ())), preferred_element_type=F32)


def _dot_tn(a, b):
    return lax.dot_general(a, b, (((0,), (0,)), ((), ())), preferred_element_type=F32)


def _rmsnorm(x, g):
    return x * lax.rsqrt(jnp.mean(x * x, axis=-1, keepdims=True) + EPS) * g


def _silu(x):
    return x * (1.0 / (1.0 + jnp.exp(-x)))


def _swiglu(xn, wg_ref, wu_ref, wd_ref, act_ref):
    for c in range(D_FF // FF_CHUNK):
        cols = slice(c * FF_CHUNK, (c + 1) * FF_CHUNK)
        g = _dot(xn, wg_ref[:, cols])
        u = _dot(xn, wu_ref[:, cols])
        act_ref[:, cols] = (_silu(g) * u).astype(BF16)
    return _dot(act_ref[...], wd_ref[...])


def _ffn_proj_kernel(x_ref, g1_ref, wg_ref, wu_ref, wd_ref, gm_ref, win_ref,
                     cosa_ref, sina_ref, sinb_ref, cosr_ref, sinr_ref,
                     h_ref, q_ref, k_ref, v_ref, rq_ref, rk_ref, rv_ref, rg_ref,
                     act_ref):
    x = x_ref[...]
    xn = _rmsnorm(x, g1_ref[...]).astype(BF16)
    h = x + 0.5 * _swiglu(xn, wg_ref, wu_ref, wd_ref, act_ref)
    h_ref[...] = h
    un = _rmsnorm(h, gm_ref[...]).astype(BF16)

    cosa, sina, sinb = cosa_ref[...], sina_ref[...], sinb_ref[...]
    cosr, sinr = cosr_ref[...], sinr_ref[...]

    def rope_a(xb):
        return (xb * cosa + pltpu.roll(xb, LANES - ROT_DIM // 2, 1) * sina
                + pltpu.roll(xb, ROT_DIM // 2, 1) * sinb)

    def rope_r(xb):
        return xb * cosr + pltpu.roll(xb, RET_HEAD_DIM // 2, 1) * sinr

    col = 0
    for c in range(ATTN_WIDTH // LANES):
        p = _dot(un, win_ref[:, col:col + LANES])
        q_ref[:, c * LANES:(c + 1) * LANES] = (rope_a(p) * (ATTN_HEAD_DIM ** -0.5)).astype(BF16)
        col += LANES
    for c in range(KV_DUP_WIDTH // LANES):
        p = _dot(un, win_ref[:, col:col + LANES])
        k_ref[:, c * LANES:(c + 1) * LANES] = rope_a(p).astype(BF16)
        col += LANES
    v_ref[...] = _dot(un, win_ref[:, col:col + KV_DUP_WIDTH]).astype(BF16)
    col += KV_DUP_WIDTH
    for c in range(N_RET_HEADS):
        p = _dot(un, win_ref[:, col:col + LANES])
        rq_ref[:, c * LANES:(c + 1) * LANES] = rope_r(p).astype(BF16)
        col += LANES
    for c in range(N_RET_HEADS):
        p = _dot(un, win_ref[:, col:col + LANES])
        rk_ref[:, c * LANES:(c + 1) * LANES] = (rope_r(p) * (RET_HEAD_DIM ** -0.5)).astype(BF16)
        col += LANES
    rv_ref[...] = _dot(un, win_ref[:, col:col + RET_WIDTH]).astype(BF16)
    col += RET_WIDTH
    rg_ref[...] = _dot(un, win_ref[:, col:col + RET_WIDTH]).astype(BF16)


def _const_spec(shape):
    return pl.BlockSpec(shape, lambda *_: (0,) * len(shape), pipeline_mode=pl.Buffered(1))


def _ffn_proj(x2d, seq_len, g1, wg, wu, wd, gm, win, tables):
    n = x2d.shape[0]
    tm = min(TOKEN_TILE, seq_len)
    tiles_per_seq = seq_len // tm
    row = lambda i: (i, 0)
    tab = lambda i: (i % tiles_per_seq, 0)
    win_cols = win.shape[1]
    out_widths = (ATTN_WIDTH, KV_DUP_WIDTH, KV_DUP_WIDTH, RET_WIDTH, RET_WIDTH, RET_WIDTH, RET_WIDTH)
    return pl.pallas_call(
        _ffn_proj_kernel,
        grid=(n // tm,),
        in_specs=[
            pl.BlockSpec((tm, D_MODEL), row),
            _const_spec((1, D_MODEL)),
            _const_spec((D_MODEL, D_FF)),
            _const_spec((D_MODEL, D_FF)),
            _const_spec((D_FF, D_MODEL)),
            _const_spec((1, D_MODEL)),
            _const_spec((D_MODEL, win_cols)),
        ] + [pl.BlockSpec((tm, LANES), tab)] * 5,
        out_specs=[pl.BlockSpec((tm, D_MODEL), row)] + [pl.BlockSpec((tm, w), row) for w in out_widths],
        out_shape=[jax.ShapeDtypeStruct((n, D_MODEL), F32)]
                  + [jax.ShapeDtypeStruct((n, w), BF16) for w in out_widths],
        scratch_shapes=[pltpu.VMEM((tm, D_FF), BF16)],
        compiler_params=pltpu.CompilerParams(dimension_semantics=("parallel",),
                                             vmem_limit_bytes=VMEM_LIMIT),
        name="ffn_proj",
    )(x2d, g1, wg, wu, wd, gm, win, *tables)


def _attn_kernel(sink_ref, q_ref, kp_ref, km_ref, kn_ref, vp_ref, vm_ref, vn_ref, gain_ref,
                 o_ref, klo_ref, khi_ref, vw_ref, bias_ref, s_ref, p_ref, r_ref, acc_ref):
    t = pl.program_id(1)
    nt = pl.num_programs(1)
    ts = q_ref.shape[1]
    nblk = ts // ATTN_BLOCK
    wk = 3 * ATTN_BLOCK

    @pl.when(t == 0)
    def _():
        qi = lax.broadcasted_iota(jnp.int32, (ATTN_BLOCK, wk), 0)
        kj = lax.broadcasted_iota(jnp.int32, (ATTN_BLOCK, wk), 1)
        band = (kj >= qi) & (kj <= qi + 2 * WINDOW)
        bias_ref[0] = jnp.where(band, 0.0, NEG_BIG)
        bias_ref[1] = jnp.where(band & (kj >= ATTN_BLOCK), 0.0, NEG_BIG)
        bias_ref[2] = jnp.where(band & (kj < 2 * ATTN_BLOCK), 0.0, NEG_BIG)

    lo_lanes = (lax.broadcasted_iota(jnp.int32, (1, KV_DUP_WIDTH), 1) % LANES) < ATTN_HEAD_DIM
    for dst, src in ((slice(0, ATTN_BLOCK), kp_ref), (slice(ATTN_BLOCK, ATTN_BLOCK + ts), km_ref),
                     (slice(ATTN_BLOCK + ts, 2 * ATTN_BLOCK + ts), kn_ref)):
        kblk = src[0]
        klo_ref[dst, :] = jnp.where(lo_lanes, kblk, jnp.zeros_like(kblk))
        khi_ref[dst, :] = jnp.where(lo_lanes, jnp.zeros_like(kblk), kblk)
    vw_ref[0:ATTN_BLOCK, :] = vp_ref[0]
    vw_ref[ATTN_BLOCK:ATTN_BLOCK + ts, :] = vm_ref[0]
    vw_ref[ATTN_BLOCK + ts:, :] = vn_ref[0]

    out_lo = lax.broadcasted_iota(jnp.int32, (ATTN_BLOCK, LANES), 1) < ATTN_HEAD_DIM
    first_idx = jnp.where(t == 0, 1, 0)
    last_idx = jnp.where(t == nt - 1, 2, 0)

    units = [(j, g, pair, half) for j in range(nblk) for g in range(N_KV_HEADS)
             for pair in range(2) for half in range(2)]

    def rows_of(j):
        return slice(j * ATTN_BLOCK, (j + 1) * ATTN_BLOCK)

    def win_rows(j):
        return slice(j * ATTN_BLOCK, j * ATTN_BLOCK + wk)

    def pair_cols(g, pair):
        return slice((2 * g + pair) * LANES, (2 * g + pair + 1) * LANES)

    for u, (j, g, pair, half) in enumerate(units):
        k_ref = khi_ref if half else klo_ref
        bias_idx = first_idx if j == 0 else (last_idx if j == nblk - 1 else 0)
        s_ref[u] = (_dot_nt(q_ref[0, rows_of(j), pair_cols(g, pair)],
                            k_ref[win_rows(j), g * LANES:(g + 1) * LANES]) + bias_ref[bias_idx])

    for u, (j, g, pair, half) in enumerate(units):
        s = s_ref[u]
        sink = sink_ref[4 * g + 2 * pair + half]
        m = jnp.maximum(jnp.max(s, axis=-1, keepdims=True), sink)
        p = jnp.exp(s - m)
        denom = jnp.sum(p, axis=-1, keepdims=True) + jnp.exp(sink - m)
        p_ref[u] = p.astype(BF16)
        r_ref[u] = jnp.broadcast_to(1.0 / denom, (ATTN_BLOCK, LANES))

    pending = []
    for u, (j, g, pair, half) in enumerate(units):
        out = _dot(p_ref[u], vw_ref[win_rows(j), g * LANES:(g + 1) * LANES]) * r_ref[u]
        if half == 0:
            pending.append(out)
        else:
            acc_ref[rows_of(j), pair_cols(g, pair)] = jnp.where(out_lo, pending.pop(), out)

    o_ref[0] = _rmsnorm(acc_ref[...], gain_ref[...]).astype(BF16)


def _attn(q, k, v, sink, gain):
    b, s, _ = q.shape
    ts = min(SEQ_TILE, s)
    nt = s // ts
    bps = ts // ATTN_BLOCK
    nb = s // ATTN_BLOCK
    main = lambda i, t: (i, t, 0)
    prev = lambda i, t: (i, jnp.maximum(t * bps - 1, 0), 0)
    nxt = lambda i, t: (i, jnp.minimum((t + 1) * bps, nb - 1), 0)
    kv_specs = [pl.BlockSpec((1, ATTN_BLOCK, KV_DUP_WIDTH), prev),
                pl.BlockSpec((1, ts, KV_DUP_WIDTH), main),
                pl.BlockSpec((1, ATTN_BLOCK, KV_DUP_WIDTH), nxt)]
    return pl.pallas_call(
        _attn_kernel,
        grid=(b, nt),
        in_specs=[pl.BlockSpec(memory_space=pltpu.SMEM),
                  pl.BlockSpec((1, ts, ATTN_WIDTH), main)] + kv_specs + kv_specs
                 + [pl.BlockSpec((1, ATTN_WIDTH), lambda i, t: (0, 0))],
        out_specs=pl.BlockSpec((1, ts, ATTN_WIDTH), main),
        out_shape=jax.ShapeDtypeStruct((b, s, ATTN_WIDTH), BF16),
        scratch_shapes=[pltpu.VMEM((ts + 2 * ATTN_BLOCK, KV_DUP_WIDTH), BF16),
                        pltpu.VMEM((ts + 2 * ATTN_BLOCK, KV_DUP_WIDTH), BF16),
                        pltpu.VMEM((ts + 2 * ATTN_BLOCK, KV_DUP_WIDTH), BF16),
                        pltpu.VMEM((3, ATTN_BLOCK, 3 * ATTN_BLOCK), F32),
                        pltpu.VMEM((bps * N_ATTN_HEADS, ATTN_BLOCK, 3 * ATTN_BLOCK), F32),
                        pltpu.VMEM((bps * N_ATTN_HEADS, ATTN_BLOCK, 3 * ATTN_BLOCK), BF16),
                        pltpu.VMEM((bps * N_ATTN_HEADS, ATTN_BLOCK, LANES), F32),
                        pltpu.VMEM((ts, ATTN_WIDTH), F32)],
        compiler_params=pltpu.CompilerParams(dimension_semantics=("parallel", "arbitrary"),
                                             vmem_limit_bytes=VMEM_LIMIT),
        name="attn",
    )(sink, q, k, k, k, v, v, v, gain)


def _head_lane_vector(ld_ref, scale):
    lane = lax.broadcasted_iota(jnp.int32, (1, RET_WIDTH), 1)
    ld = jnp.zeros((1, RET_WIDTH), F32)
    for h in range(N_RET_HEADS):
        ld = jnp.where(lane >= h * RET_HEAD_DIM, ld_ref[h], ld)
    return jnp.exp(ld * scale)


def _ret_state_kernel(ldb_ref, rk_ref, rv_ref, sb_ref, state_ref):
    t = pl.program_id(1)
    nchunk = rk_ref.shape[1] // RET_CHUNK

    @pl.when(t == 0)
    def _():
        state_ref[...] = jnp.zeros_like(state_ref)

    row = lax.broadcasted_iota(jnp.int32, (RET_CHUNK, RET_HEAD_DIM), 0).astype(F32)
    chunk_decay = _head_lane_vector(ldb_ref, float(RET_CHUNK))
    for c in reversed(range(nchunk)):
        rows = slice(c * RET_CHUNK, (c + 1) * RET_CHUNK)
        sb_ref[0, c] = state_ref[...].astype(BF16)
        kvs = []
        for h in range(N_RET_HEADS):
            cols = slice(h * RET_HEAD_DIM, (h + 1) * RET_HEAD_DIM)
            kb = (rk_ref[0, rows, cols].astype(F32) * jnp.exp(ldb_ref[h] * row)).astype(BF16)
            kvs.append(_dot_tn(kb, rv_ref[0, rows, cols]))
        state_ref[...] = state_ref[...] * chunk_decay + jnp.concatenate(kvs, axis=1)


def _ret_state(rk, rv, ld_b):
    b, s, _ = rk.shape
    ts = min(SEQ_TILE, s)
    nt = s // ts
    cpt = ts // RET_CHUNK
    rev = lambda i, t: (i, nt - 1 - t, 0)
    return pl.pallas_call(
        _ret_state_kernel,
        grid=(b, nt),
        in_specs=[pl.BlockSpec(memory_space=pltpu.SMEM),
                  pl.BlockSpec((1, ts, RET_WIDTH), rev),
                  pl.BlockSpec((1, ts, RET_WIDTH), rev)],
        out_specs=pl.BlockSpec((1, cpt, RET_HEAD_DIM, RET_WIDTH), lambda i, t: (i, nt - 1 - t, 0, 0)),
        out_shape=jax.ShapeDtypeStruct((b, s // RET_CHUNK, RET_HEAD_DIM, RET_WIDTH), BF16),
        scratch_shapes=[pltpu.VMEM((RET_HEAD_DIM, RET_WIDTH), F32)],
        compiler_params=pltpu.CompilerParams(dimension_semantics=("parallel", "arbitrary"),
                                             vmem_limit_bytes=VMEM_LIMIT),
        name="ret_state",
    )(ld_b, rk, rv)


def _ret_out_kernel(ldf_ref, ldb_ref, rq_ref, rk_ref, rv_ref, rg_ref, sb_ref, o_ref,
                    state_ref, mask_ref):
    t = pl.program_id(1)
    nchunk = rq_ref.shape[1] // RET_CHUNK
    shape = (RET_CHUNK, RET_HEAD_DIM)
    row_i = lax.broadcasted_iota(jnp.int32, shape, 0)
    col_i = lax.broadcasted_iota(jnp.int32, shape, 1)
    row = row_i.astype(F32)

    @pl.when(t == 0)
    def _():
        state_ref[...] = jnp.zeros_like(state_ref)
        diff = (row_i - col_i).astype(F32)
        for h in range(N_RET_HEADS):
            d_fwd = jnp.where(diff >= 0, jnp.exp(ldf_ref[h] * jnp.maximum(diff, 0.0)), 0.0)
            d_bwd = jnp.where(diff <= 0, jnp.exp(ldb_ref[h] * jnp.maximum(-diff, 0.0)), 0.0)
            mask_ref[h] = d_fwd + d_bwd

    chunk_decay = _head_lane_vector(ldf_ref, float(RET_CHUNK))
    for c in range(nchunk):
        rows = slice(c * RET_CHUNK, (c + 1) * RET_CHUNK)
        kvs = []
        for h in range(N_RET_HEADS):
            cols = slice(h * RET_HEAD_DIM, (h + 1) * RET_HEAD_DIM)
            q = rq_ref[0, rows, cols]
            k = rk_ref[0, rows, cols]
            v = rv_ref[0, rows, cols]
            inner = (_dot_nt(q, k) * mask_ref[h]).astype(BF16)
            o = _dot(inner, v)
            o += jnp.exp(ldf_ref[h] * (row + 1.0)) * _dot(q, state_ref[:, cols].astype(BF16))
            o += jnp.exp(ldb_ref[h] * (float(RET_CHUNK) - row)) * _dot(q, sb_ref[0, c, :, cols])
            mu = jnp.mean(o, axis=-1, keepdims=True)
            var = jnp.mean(jnp.square(o - mu), axis=-1, keepdims=True)
            on = (o - mu) * lax.rsqrt(var + EPS)
            o_ref[0, rows, cols] = (_silu(rg_ref[0, rows, cols].astype(F32)) * on).astype(BF16)
            kf = (k.astype(F32) * jnp.exp(ldf_ref[h] * (float(RET_CHUNK) - 1.0 - row))).astype(BF16)
            kvs.append(_dot_tn(kf, v))
        state_ref[...] = state_ref[...] * chunk_decay + jnp.concatenate(kvs, axis=1)


def _ret_out(rq, rk, rv, rg, sb, ld_f, ld_b):
    b, s, _ = rq.shape
    ts = min(SEQ_TILE, s)
    nt = s // ts
    cpt = ts // RET_CHUNK
    main = lambda i, t: (i, t, 0)
    seq_spec = pl.BlockSpec((1, ts, RET_WIDTH), main)
    return pl.pallas_call(
        _ret_out_kernel,
        grid=(b, nt),
        in_specs=[pl.BlockSpec(memory_space=pltpu.SMEM), pl.BlockSpec(memory_space=pltpu.SMEM),
                  seq_spec, seq_spec, seq_spec, seq_spec,
                  pl.BlockSpec((1, cpt, RET_HEAD_DIM, RET_WIDTH), lambda i, t: (i, t, 0, 0))],
        out_specs=seq_spec,
        out_shape=jax.ShapeDtypeStruct((b, s, RET_WIDTH), BF16),
        scratch_shapes=[pltpu.VMEM((RET_HEAD_DIM, RET_WIDTH), F32),
                        pltpu.VMEM((N_RET_HEADS, RET_CHUNK, RET_HEAD_DIM), F32)],
        compiler_params=pltpu.CompilerParams(dimension_semantics=("parallel", "arbitrary"),
                                             vmem_limit_bytes=VMEM_LIMIT),
        name="ret_out",
    )(ld_f, ld_b, rq, rk, rv, rg, sb)


def _out_ffn_kernel(h_ref, a_ref, r_ref, woa_ref, wor_ref, g2_ref, wg_ref, wu_ref, wd_ref, gf_ref,
                    y_ref, act_ref):
    h = h_ref[...] + _dot(a_ref[...], woa_ref[...]) + _dot(r_ref[...], wor_ref[...])
    hn = _rmsnorm(h, g2_ref[...]).astype(BF16)
    h = h + 0.5 * _swiglu(hn, wg_ref, wu_ref, wd_ref, act_ref)
    y_ref[...] = _rmsnorm(h, gf_ref[...])


def _out_ffn(h, a, r, woa, wor, g2, wg, wu, wd, gf):
    n = h.shape[0]
    tm = min(TOKEN_TILE, n)
    row = lambda i: (i, 0)
    return pl.pallas_call(
        _out_ffn_kernel,
        grid=(n // tm,),
        in_specs=[
            pl.BlockSpec((tm, D_MODEL), row),
            pl.BlockSpec((tm, ATTN_WIDTH), row),
            pl.BlockSpec((tm, RET_WIDTH), row),
            _const_spec((ATTN_WIDTH, D_MODEL)),
            _const_spec((RET_WIDTH, D_MODEL)),
            _const_spec((1, D_MODEL)),
            _const_spec((D_MODEL, D_FF)),
            _const_spec((D_MODEL, D_FF)),
            _const_spec((D_FF, D_MODEL)),
            _const_spec((1, D_MODEL)),
        ],
        out_specs=pl.BlockSpec((tm, D_MODEL), row),
        out_shape=jax.ShapeDtypeStruct((n, D_MODEL), F32),
        scratch_shapes=[pltpu.VMEM((tm, D_FF), BF16)],
        compiler_params=pltpu.CompilerParams(dimension_semantics=("parallel",),
                                             vmem_limit_bytes=VMEM_LIMIT),
        name="out_ffn",
    )(h, a, r, woa, wor, g2, wg, wu, wd, gf)


def _rope_tables(seq_len):
    pos = jnp.arange(seq_len, dtype=F32)
    inv_a = ROPE_THETA ** (-jnp.arange(0, ROT_DIM, 2, dtype=F32) / ROT_DIM)
    ang_a = pos[:, None] * inv_a[None, :]
    cos8, sin8 = jnp.cos(ang_a), jnp.sin(ang_a)
    half = ROT_DIM // 2
    pad = ATTN_HEAD_DIM - ROT_DIM
    ones = jnp.ones((seq_len, pad), F32)
    zeros = jnp.zeros((seq_len, pad), F32)
    z8 = jnp.zeros((seq_len, half), F32)
    cos_a = jnp.concatenate([cos8, cos8, ones], axis=1)
    sin_a = jnp.concatenate([-sin8, z8, zeros], axis=1)
    sin_b = jnp.concatenate([z8, sin8, zeros], axis=1)
    rep = LANES // ATTN_HEAD_DIM
    cos_a, sin_a, sin_b = (jnp.tile(t, (1, rep)) for t in (cos_a, sin_a, sin_b))
    inv_r = RET_ROT_THETA ** (-jnp.linspace(0.0, 1.0, RET_HEAD_DIM // 2, dtype=F32))
    ang_r = pos[:, None] * inv_r[None, :]
    cos_r = jnp.concatenate([jnp.cos(ang_r)] * 2, axis=1)
    sin_r = jnp.concatenate([-jnp.sin(ang_r), jnp.sin(ang_r)], axis=1)
    return cos_a, sin_a, sin_b, cos_r, sin_r


def _dup_kv_columns(w):
    heads = [w[:, h * ATTN_HEAD_DIM:(h + 1) * ATTN_HEAD_DIM] for h in range(N_KV_HEADS)]
    return jnp.concatenate([heads[0], heads[0], heads[1], heads[1]], axis=1)


def _layer(x, p):
    b, s, _ = x.shape
    n = b * s
    tables = _rope_tables(s)
    h, q, k, v, rq, rk, rv, rg = _ffn_proj(
        x.reshape(n, D_MODEL), s, p["g1"], p["wg1"], p["wu1"], p["wd1"], p["gm"], p["win"], tables)
    seq = lambda a: a.reshape(b, s, a.shape[-1])
    attn = _attn(seq(q), seq(k), seq(v), p["sink"], p["ga"])
    sb = _ret_state(seq(rk), seq(rv), p["ldb"])
    ret = _ret_out(seq(rq), seq(rk), seq(rv), seq(rg), sb, p["ldf"], p["ldb"])
    y = _out_ffn(h, attn.reshape(n, ATTN_WIDTH), ret.reshape(n, RET_WIDTH),
                 p["woa"], p["wor"], p["g2"], p["wg2"], p["wu2"], p["wd2"], p["gf"])
    return y.reshape(b, s, D_MODEL)


def kernel(x_prompt, x_sample, ffn1_norm, ffn1_w_gate, ffn1_w_up, ffn1_w_down, mix_norm, w_in, attn_sink, attn_out_norm, ret_log_decay_fwd, ret_log_decay_bwd, w_out, ffn2_norm, ffn2_w_gate, ffn2_w_up, ffn2_w_down, final_norm):
    assert ffn1_norm.shape[0] == 1
    o1 = ATTN_WIDTH
    o2 = o1 + N_KV_HEADS * ATTN_HEAD_DIM
    o3 = o2 + N_KV_HEADS * ATTN_HEAD_DIM
    wi = w_in[0]
    win = jnp.concatenate([wi[:, :o1], _dup_kv_columns(wi[:, o1:o2]), _dup_kv_columns(wi[:, o2:o3]),
                           wi[:, o3:]], axis=1).astype(BF16)
    wo = w_out[0].astype(BF16)
    p = dict(
        g1=ffn1_norm[0][None, :], wg1=ffn1_w_gate[0].astype(BF16), wu1=ffn1_w_up[0].astype(BF16),
        wd1=ffn1_w_down[0].astype(BF16), gm=mix_norm[0][None, :], win=win,
        sink=attn_sink[0], ga=attn_out_norm[0][None, :],
        ldf=ret_log_decay_fwd[0], ldb=ret_log_decay_bwd[0],
        woa=wo[:ATTN_WIDTH], wor=wo[ATTN_WIDTH:],
        g2=ffn2_norm[0][None, :], wg2=ffn2_w_gate[0].astype(BF16), wu2=ffn2_w_up[0].astype(BF16),
        wd2=ffn2_w_down[0].astype(BF16), gf=final_norm[None, :],
    )
    return (_layer(x_prompt, p), _layer(x_sample, p))
```

```python
import functools

import jax
import jax.numpy as jnp
from jax import lax
from jax.experimental import pallas as pl
from jax.experimental.pallas import tpu as pltpu

D_MODEL = 1024
D_FF = 2816
N_ATTN_HEADS = 8
N_KV_HEADS = 2
ATTN_HEAD_DIM = 64
ATTN_WIDTH = N_ATTN_HEADS * ATTN_HEAD_DIM
WINDOW = 128
ATTN_BLOCK = 128
ROT_DIM = 16
ROPE_THETA = 500000.0
RET_HEAD_DIM = 128
N_RET_HEADS = 4
RET_WIDTH = N_RET_HEADS * RET_HEAD_DIM
RET_CHUNK = 128
RET_ROT_THETA = 10000.0
EPS = 1e-6
NEG_BIG = -1e30

LANES = 128
KV_DUP_WIDTH = 2 * LANES
FF_CHUNK = 256
PROJ_CHUNK = 512
TOKEN_TILE = 512
SEQ_TILE = 512
RET_STATE_TILE = 2048
VMEM_LIMIT = 56 * 1024 * 1024

BF16 = jnp.bfloat16
F32 = jnp.float32


def _dot(a, b):
    return jnp.dot(a, b, preferred_element_type=F32)


def _dot_nt(a, b):
    return lax.dot_general(a, b, (((1,), (1,)), ((), ())), preferred_element_type=F32)


def _dot_tn(a, b):
    return lax.dot_general(a, b, (((0,), (0,)), ((), ())), preferred_element_type=F32)


def _rmsnorm(x, g):
    return x * lax.rsqrt(jnp.mean(x * x, axis=-1, keepdims=True) + EPS) * g


def _silu(x):
    return x * (1.0 / (1.0 + jnp.exp(-x)))


def _swiglu(xn, wg_ref, wu_ref, wd_ref, act_ref):
    for c in range(D_FF // FF_CHUNK):
        cols = slice(c * FF_CHUNK, (c + 1) * FF_CHUNK)
        g = _dot(xn, wg_ref[:, cols])
        u = _dot(xn, wu_ref[:, cols])
        act_ref[:, cols] = (_silu(g) * u).astype(BF16)
    return _dot(act_ref[...], wd_ref[...])


def _ffn_proj_kernel(x_ref, g1_ref, wg_ref, wu_ref, wd_ref, gm_ref, win_ref,
                     cosa_ref, sina_ref, sinb_ref, cosr_ref, sinr_ref,
                     h_ref, q_ref, k_ref, v_ref, rq_ref, rk_ref, rv_ref, rg_ref,
                     act_ref):
    x = x_ref[...]
    xn = _rmsnorm(x, g1_ref[...]).astype(BF16)
    h = x + 0.5 * _swiglu(xn, wg_ref, wu_ref, wd_ref, act_ref)
    h_ref[...] = h
    un = _rmsnorm(h, gm_ref[...]).astype(BF16)

    cosa, sina, sinb = cosa_ref[...], sina_ref[...], sinb_ref[...]
    cosr, sinr = cosr_ref[...], sinr_ref[...]

    def rope_a(xb):
        return (xb * cosa + pltpu.roll(xb, LANES - ROT_DIM // 2, 1) * sina
                + pltpu.roll(xb, ROT_DIM // 2, 1) * sinb)

    def rope_r(xb):
        return xb * cosr + pltpu.roll(xb, RET_HEAD_DIM // 2, 1) * sinr

    def lane_blocks(p):
        return [p[:, c * LANES:(c + 1) * LANES] for c in range(p.shape[1] // LANES)]

    def proj(section):
        return _dot(un, win_ref[:, section * PROJ_CHUNK:(section + 1) * PROJ_CHUNK])

    for c, blk in enumerate(lane_blocks(proj(0))):
        q_ref[:, c * LANES:(c + 1) * LANES] = (rope_a(blk) * (ATTN_HEAD_DIM ** -0.5)).astype(BF16)
    kv = lane_blocks(proj(1))
    for c, blk in enumerate(kv[:KV_DUP_WIDTH // LANES]):
        k_ref[:, c * LANES:(c + 1) * LANES] = rope_a(blk).astype(BF16)
    for c, blk in enumerate(kv[KV_DUP_WIDTH // LANES:]):
        v_ref[:, c * LANES:(c + 1) * LANES] = blk.astype(BF16)
    for c, blk in enumerate(lane_blocks(proj(2))):
        rq_ref[:, c * LANES:(c + 1) * LANES] = rope_r(blk).astype(BF16)
    for c, blk in enumerate(lane_blocks(proj(3))):
        rk_ref[:, c * LANES:(c + 1) * LANES] = (rope_r(blk) * (RET_HEAD_DIM ** -0.5)).astype(BF16)
    rv_ref[...] = proj(4).astype(BF16)
    rg_ref[...] = proj(5).astype(BF16)


def _const_spec(shape):
    return pl.BlockSpec(shape, lambda *_: (0,) * len(shape), pipeline_mode=pl.Buffered(1))


def _ffn_proj(x2d, seq_len, g1, wg, wu, wd, gm, win, tables):
    n = x2d.shape[0]
    tm = min(TOKEN_TILE, seq_len)
    tiles_per_seq = seq_len // tm
    row = lambda i: (i, 0)
    tab = lambda i: (i % tiles_per_seq, 0)
    win_cols = win.shape[1]
    out_widths = (ATTN_WIDTH, KV_DUP_WIDTH, KV_DUP_WIDTH, RET_WIDTH, RET_WIDTH, RET_WIDTH, RET_WIDTH)
    return pl.pallas_call(
        _ffn_proj_kernel,
        grid=(n // tm,),
        in_specs=[
            pl.BlockSpec((tm, D_MODEL), row),
            _const_spec((1, D_MODEL)),
            _const_spec((D_MODEL, D_FF)),
            _const_spec((D_MODEL, D_FF)),
            _const_spec((D_FF, D_MODEL)),
            _const_spec((1, D_MODEL)),
            _const_spec((D_MODEL, win_cols)),
        ] + [pl.BlockSpec((tm, LANES), tab)] * 5,
        out_specs=[pl.BlockSpec((tm, D_MODEL), row)] + [pl.BlockSpec((tm, w), row) for w in out_widths],
        out_shape=[jax.ShapeDtypeStruct((n, D_MODEL), F32)]
                  + [jax.ShapeDtypeStruct((n, w), BF16) for w in out_widths],
        scratch_shapes=[pltpu.VMEM((tm, D_FF), BF16)],
        compiler_params=pltpu.CompilerParams(dimension_semantics=("parallel",),
                                             vmem_limit_bytes=VMEM_LIMIT),
        name="ffn_proj",
    )(x2d, g1, wg, wu, wd, gm, win, *tables)


def _attn_kernel(sink_ref, q_ref, kp_ref, km_ref, kn_ref, vp_ref, vm_ref, vn_ref, gain_ref,
                 o_ref, klo_ref, khi_ref, vw_ref, bias_ref, s_ref, p_ref, r_ref, acc_ref):
    t = pl.program_id(1)
    nt = pl.num_programs(1)
    ts = q_ref.shape[1]
    nblk = ts // ATTN_BLOCK
    wk = 3 * ATTN_BLOCK

    @pl.when(t == 0)
    def _():
        qi = lax.broadcasted_iota(jnp.int32, (ATTN_BLOCK, wk), 0)
        kj = lax.broadcasted_iota(jnp.int32, (ATTN_BLOCK, wk), 1)
        band = (kj >= qi) & (kj <= qi + 2 * WINDOW)
        bias_ref[0] = jnp.where(band, 0.0, NEG_BIG)
        bias_ref[1] = jnp.where(band & (kj >= ATTN_BLOCK), 0.0, NEG_BIG)
        bias_ref[2] = jnp.where(band & (kj < 2 * ATTN_BLOCK), 0.0, NEG_BIG)

    lo_lanes = (lax.broadcasted_iota(jnp.int32, (1, KV_DUP_WIDTH), 1) % LANES) < ATTN_HEAD_DIM
    for dst, src in ((slice(0, ATTN_BLOCK), kp_ref), (slice(ATTN_BLOCK, ATTN_BLOCK + ts), km_ref),
                     (slice(ATTN_BLOCK + ts, 2 * ATTN_BLOCK + ts), kn_ref)):
        kblk = src[0]
        klo_ref[dst, :] = jnp.where(lo_lanes, kblk, jnp.zeros_like(kblk))
        khi_ref[dst, :] = jnp.where(lo_lanes, jnp.zeros_like(kblk), kblk)
    vw_ref[0:ATTN_BLOCK, :] = vp_ref[0]
    vw_ref[ATTN_BLOCK:ATTN_BLOCK + ts, :] = vm_ref[0]
    vw_ref[ATTN_BLOCK + ts:, :] = vn_ref[0]

    out_lo = lax.broadcasted_iota(jnp.int32, (ATTN_BLOCK, LANES), 1) < ATTN_HEAD_DIM
    first_idx = jnp.where(t == 0, 1, 0)
    last_idx = jnp.where(t == nt - 1, 2, 0)

    units = [(j, g, pair, half) for j in range(nblk) for g in range(N_KV_HEADS)
             for pair in range(2) for half in range(2)]

    def rows_of(j):
        return slice(j * ATTN_BLOCK, (j + 1) * ATTN_BLOCK)

    def win_rows(j):
        return slice(j * ATTN_BLOCK, j * ATTN_BLOCK + wk)

    def pair_cols(g, pair):
        return slice((2 * g + pair) * LANES, (2 * g + pair + 1) * LANES)

    for u, (j, g, pair, half) in enumerate(units):
        k_ref = khi_ref if half else klo_ref
        bias_idx = first_idx if j == 0 else (last_idx if j == nblk - 1 else 0)
        s_ref[u] = (_dot_nt(q_ref[0, rows_of(j), pair_cols(g, pair)],
                            k_ref[win_rows(j), g * LANES:(g + 1) * LANES]) + bias_ref[bias_idx])

    for u, (j, g, pair, half) in enumerate(units):
        s = s_ref[u]
        sink = sink_ref[4 * g + 2 * pair + half]
        m = jnp.maximum(jnp.max(s, axis=-1, keepdims=True), sink)
        p = jnp.exp(s - m)
        denom = jnp.sum(p, axis=-1, keepdims=True) + jnp.exp(sink - m)
        p_ref[u] = p.astype(BF16)
        r_ref[u] = jnp.broadcast_to(1.0 / denom, (ATTN_BLOCK, LANES))

    pending = []
    for u, (j, g, pair, half) in enumerate(units):
        out = _dot(p_ref[u], vw_ref[win_rows(j), g * LANES:(g + 1) * LANES]) * r_ref[u]
        if half == 0:
            pending.append(out)
        else:
            acc_ref[rows_of(j), pair_cols(g, pair)] = jnp.where(out_lo, pending.pop(), out)

    o_ref[0] = _rmsnorm(acc_ref[...], gain_ref[...]).astype(BF16)


def _attn(q, k, v, sink, gain):
    b, s, _ = q.shape
    ts = min(SEQ_TILE, s)
    nt = s // ts
    bps = ts // ATTN_BLOCK
    nb = s // ATTN_BLOCK
    main = lambda i, t: (i, t, 0)
    prev = lambda i, t: (i, jnp.maximum(t * bps - 1, 0), 0)
    nxt = lambda i, t: (i, jnp.minimum((t + 1) * bps, nb - 1), 0)
    kv_specs = [pl.BlockSpec((1, ATTN_BLOCK, KV_DUP_WIDTH), prev),
                pl.BlockSpec((1, ts, KV_DUP_WIDTH), main),
                pl.BlockSpec((1, ATTN_BLOCK, KV_DUP_WIDTH), nxt)]
    return pl.pallas_call(
        _attn_kernel,
        grid=(b, nt),
        in_specs=[pl.BlockSpec(memory_space=pltpu.SMEM),
                  pl.BlockSpec((1, ts, ATTN_WIDTH), main)] + kv_specs + kv_specs
                 + [pl.BlockSpec((1, ATTN_WIDTH), lambda i, t: (0, 0))],
        out_specs=pl.BlockSpec((1, ts, ATTN_WIDTH), main),
        out_shape=jax.ShapeDtypeStruct((b, s, ATTN_WIDTH), BF16),
        scratch_shapes=[pltpu.VMEM((ts + 2 * ATTN_BLOCK, KV_DUP_WIDTH), BF16),
                        pltpu.VMEM((ts + 2 * ATTN_BLOCK, KV_DUP_WIDTH), BF16),
                        pltpu.VMEM((ts + 2 * ATTN_BLOCK, KV_DUP_WIDTH), BF16),
                        pltpu.VMEM((3, ATTN_BLOCK, 3 * ATTN_BLOCK), F32),
                        pltpu.VMEM((bps * N_ATTN_HEADS, ATTN_BLOCK, 3 * ATTN_BLOCK), F32),
                        pltpu.VMEM((bps * N_ATTN_HEADS, ATTN_BLOCK, 3 * ATTN_BLOCK), BF16),
                        pltpu.VMEM((bps * N_ATTN_HEADS, ATTN_BLOCK, LANES), F32),
                        pltpu.VMEM((ts, ATTN_WIDTH), F32)],
        compiler_params=pltpu.CompilerParams(dimension_semantics=("parallel", "arbitrary"),
                                             vmem_limit_bytes=VMEM_LIMIT),
        name="attn",
    )(sink, q, k, k, k, v, v, v, gain)


def _head_lane_vector(ld_ref, scale):
    lane = lax.broadcasted_iota(jnp.int32, (1, RET_WIDTH), 1)
    ld = jnp.zeros((1, RET_WIDTH), F32)
    for h in range(N_RET_HEADS):
        ld = jnp.where(lane >= h * RET_HEAD_DIM, ld_ref[h], ld)
    return jnp.exp(ld * scale)


def _head_cols(h):
    return slice(h * RET_HEAD_DIM, (h + 1) * RET_HEAD_DIM)


def _chunk_rows(c):
    return slice(c * RET_CHUNK, (c + 1) * RET_CHUNK)


def _ret_state_kernel(ldb_ref, rk_ref, rv_ref, sb_ref, state_ref, tab_ref, kv_ref):
    t = pl.program_id(1)
    nchunk = rk_ref.shape[1] // RET_CHUNK

    @pl.when(t == 0)
    def _():
        state_ref[...] = jnp.zeros_like(state_ref)
        row = lax.broadcasted_iota(jnp.int32, (RET_CHUNK, RET_HEAD_DIM), 0).astype(F32)
        for h in range(N_RET_HEADS):
            tab_ref[h] = jnp.exp(ldb_ref[h] * row)

    for c in range(nchunk):
        for h in range(N_RET_HEADS):
            kb = (rk_ref[0, _chunk_rows(c), _head_cols(h)].astype(F32) * tab_ref[h]).astype(BF16)
            kv_ref[c, :, _head_cols(h)] = _dot_tn(kb, rv_ref[0, _chunk_rows(c), _head_cols(h)])

    chunk_decay = _head_lane_vector(ldb_ref, float(RET_CHUNK))
    state = state_ref[...]
    for c in reversed(range(nchunk)):
        sb_ref[0, c] = state.astype(BF16)
        state = state * chunk_decay + kv_ref[c]
    state_ref[...] = state


def _ret_state(rk, rv, ld_b):
    b, s, _ = rk.shape
    ts = min(RET_STATE_TILE, s)
    nt = s // ts
    cpt = ts // RET_CHUNK
    rev = lambda i, t: (i, nt - 1 - t, 0)
    return pl.pallas_call(
        _ret_state_kernel,
        grid=(b, nt),
        in_specs=[pl.BlockSpec(memory_space=pltpu.SMEM),
                  pl.BlockSpec((1, ts, RET_WIDTH), rev),
                  pl.BlockSpec((1, ts, RET_WIDTH), rev)],
        out_specs=pl.BlockSpec((1, cpt, RET_HEAD_DIM, RET_WIDTH), lambda i, t: (i, nt - 1 - t, 0, 0)),
        out_shape=jax.ShapeDtypeStruct((b, s // RET_CHUNK, RET_HEAD_DIM, RET_WIDTH), BF16),
        scratch_shapes=[pltpu.VMEM((RET_HEAD_DIM, RET_WIDTH), F32),
                        pltpu.VMEM((N_RET_HEADS, RET_CHUNK, RET_HEAD_DIM), F32),
                        pltpu.VMEM((cpt, RET_HEAD_DIM, RET_WIDTH), F32)],
        compiler_params=pltpu.CompilerParams(dimension_semantics=("parallel", "arbitrary"),
                                             vmem_limit_bytes=VMEM_LIMIT),
        name="ret_state",
    )(ld_b, rk, rv)


def _ret_out_kernel(ldf_ref, ldb_ref, rq_ref, rk_ref, rv_ref, rg_ref, sb_ref, o_ref,
                    state_ref, tab_ref, inner_ref, kv_ref, cs_ref):
    t = pl.program_id(1)
    nchunk = rq_ref.shape[1] // RET_CHUNK
    MASK, Q_FWD, Q_BWD, K_FWD = range(4)

    @pl.when(t == 0)
    def _():
        state_ref[...] = jnp.zeros_like(state_ref)
        shape = (RET_CHUNK, RET_HEAD_DIM)
        row_i = lax.broadcasted_iota(jnp.int32, shape, 0)
        col_i = lax.broadcasted_iota(jnp.int32, shape, 1)
        row = row_i.astype(F32)
        diff = (row_i - col_i).astype(F32)
        for h in range(N_RET_HEADS):
            d_fwd = jnp.where(diff >= 0, jnp.exp(ldf_ref[h] * jnp.maximum(diff, 0.0)), 0.0)
            d_bwd = jnp.where(diff <= 0, jnp.exp(ldb_ref[h] * jnp.maximum(-diff, 0.0)), 0.0)
            tab_ref[MASK, h] = d_fwd + d_bwd
            tab_ref[Q_FWD, h] = jnp.exp(ldf_ref[h] * (row + 1.0))
            tab_ref[Q_BWD, h] = jnp.exp(ldb_ref[h] * (float(RET_CHUNK) - row))
            tab_ref[K_FWD, h] = jnp.exp(ldf_ref[h] * (float(RET_CHUNK) - 1.0 - row))

    units = [(c, h) for c in range(nchunk) for h in range(N_RET_HEADS)]

    for u, (c, h) in enumerate(units):
        q = rq_ref[0, _chunk_rows(c), _head_cols(h)]
        k = rk_ref[0, _chunk_rows(c), _head_cols(h)]
        inner_ref[u] = (_dot_nt(q, k) * tab_ref[MASK, h]).astype(BF16)
        kf = (k.astype(F32) * tab_ref[K_FWD, h]).astype(BF16)
        kv_ref[c, :, _head_cols(h)] = _dot_tn(kf, rv_ref[0, _chunk_rows(c), _head_cols(h)])

    chunk_decay = _head_lane_vector(ldf_ref, float(RET_CHUNK))
    state = state_ref[...]
    for c in range(nchunk):
        fwd = state.astype(BF16)
        for h in range(N_RET_HEADS):
            base = 2 * h * RET_HEAD_DIM
            cs_ref[c, :, base:base + RET_HEAD_DIM] = fwd[:, _head_cols(h)]
            cs_ref[c, :, base + RET_HEAD_DIM:base + 2 * RET_HEAD_DIM] = sb_ref[0, c, :, _head_cols(h)]
        state = state * chunk_decay + kv_ref[c]
    state_ref[...] = state

    for u, (c, h) in enumerate(units):
        q = rq_ref[0, _chunk_rows(c), _head_cols(h)]
        o = _dot(inner_ref[u], rv_ref[0, _chunk_rows(c), _head_cols(h)])
        cross = _dot(q, cs_ref[c, :, 2 * h * RET_HEAD_DIM:2 * (h + 1) * RET_HEAD_DIM])
        o += tab_ref[Q_FWD, h] * cross[:, :RET_HEAD_DIM] + tab_ref[Q_BWD, h] * cross[:, RET_HEAD_DIM:]
        mu = jnp.mean(o, axis=-1, keepdims=True)
        var = jnp.mean(jnp.square(o - mu), axis=-1, keepdims=True)
        on = (o - mu) * lax.rsqrt(var + EPS)
        gate = _silu(rg_ref[0, _chunk_rows(c), _head_cols(h)].astype(F32))
        o_ref[0, _chunk_rows(c), _head_cols(h)] = (gate * on).astype(BF16)


def _ret_out(rq, rk, rv, rg, sb, ld_f, ld_b):
    b, s, _ = rq.shape
    ts = min(SEQ_TILE, s)
    nt = s // ts
    cpt = ts // RET_CHUNK
    main = lambda i, t: (i, t, 0)
    seq_spec = pl.BlockSpec((1, ts, RET_WIDTH), main)
    return pl.pallas_call(
        _ret_out_kernel,
        grid=(b, nt),
        in_specs=[pl.BlockSpec(memory_space=pltpu.SMEM), pl.BlockSpec(memory_space=pltpu.SMEM),
                  seq_spec, seq_spec, seq_spec, seq_spec,
                  pl.BlockSpec((1, cpt, RET_HEAD_DIM, RET_WIDTH), lambda i, t: (i, t, 0, 0))],
        out_specs=seq_spec,
        out_shape=jax.ShapeDtypeStruct((b, s, RET_WIDTH), BF16),
        scratch_shapes=[pltpu.VMEM((RET_HEAD_DIM, RET_WIDTH), F32),
                        pltpu.VMEM((4, N_RET_HEADS, RET_CHUNK, RET_HEAD_DIM), F32),
                        pltpu.VMEM((cpt * N_RET_HEADS, RET_CHUNK, RET_CHUNK), BF16),
                        pltpu.VMEM((cpt, RET_HEAD_DIM, RET_WIDTH), F32),
                        pltpu.VMEM((cpt, RET_HEAD_DIM, 2 * RET_WIDTH), BF16)],
        compiler_params=pltpu.CompilerParams(dimension_semantics=("parallel", "arbitrary"),
                                             vmem_limit_bytes=VMEM_LIMIT),
        name="ret_out",
    )(ld_f, ld_b, rq, rk, rv, rg, sb)


def _out_ffn_kernel(h_ref, a_ref, r_ref, woa_ref, wor_ref, g2_ref, wg_ref, wu_ref, wd_ref, gf_ref,
                    y_ref, act_ref):
    h = h_ref[...] + _dot(a_ref[...], woa_ref[...]) + _dot(r_ref[...], wor_ref[...])
    hn = _rmsnorm(h, g2_ref[...]).astype(BF16)
    h = h + 0.5 * _swiglu(hn, wg_ref, wu_ref, wd_ref, act_ref)
    y_ref[...] = _rmsnorm(h, gf_ref[...])


def _out_ffn(h, a, r, woa, wor, g2, wg, wu, wd, gf):
    n = h.shape[0]
    tm = min(TOKEN_TILE, n)
    row = lambda i: (i, 0)
    return pl.pallas_call(
        _out_ffn_kernel,
        grid=(n // tm,),
        in_specs=[
            pl.BlockSpec((tm, D_MODEL), row),
            pl.BlockSpec((tm, ATTN_WIDTH), row),
            pl.BlockSpec((tm, RET_WIDTH), row),
            _const_spec((ATTN_WIDTH, D_MODEL)),
            _const_spec((RET_WIDTH, D_MODEL)),
            _const_spec((1, D_MODEL)),
            _const_spec((D_MODEL, D_FF)),
            _const_spec((D_MODEL, D_FF)),
            _const_spec((D_FF, D_MODEL)),
            _const_spec((1, D_MODEL)),
        ],
        out_specs=pl.BlockSpec((tm, D_MODEL), row),
        out_shape=jax.ShapeDtypeStruct((n, D_MODEL), F32),
        scratch_shapes=[pltpu.VMEM((tm, D_FF), BF16)],
        compiler_params=pltpu.CompilerParams(dimension_semantics=("parallel",),
                                             vmem_limit_bytes=VMEM_LIMIT),
        name="out_ffn",
    )(h, a, r, woa, wor, g2, wg, wu, wd, gf)


def _rope_tables(seq_len):
    pos = jnp.arange(seq_len, dtype=F32)
    inv_a = ROPE_THETA ** (-jnp.arange(0, ROT_DIM, 2, dtype=F32) / ROT_DIM)
    ang_a = pos[:, None] * inv_a[None, :]
    cos8, sin8 = jnp.cos(ang_a), jnp.sin(ang_a)
    half = ROT_DIM // 2
    pad = ATTN_HEAD_DIM - ROT_DIM
    ones = jnp.ones((seq_len, pad), F32)
    zeros = jnp.zeros((seq_len, pad), F32)
    z8 = jnp.zeros((seq_len, half), F32)
    cos_a = jnp.concatenate([cos8, cos8, ones], axis=1)
    sin_a = jnp.concatenate([-sin8, z8, zeros], axis=1)
    sin_b = jnp.concatenate([z8, sin8, zeros], axis=1)
    rep = LANES // ATTN_HEAD_DIM
    cos_a, sin_a, sin_b = (jnp.tile(t, (1, rep)) for t in (cos_a, sin_a, sin_b))
    inv_r = RET_ROT_THETA ** (-jnp.linspace(0.0, 1.0, RET_HEAD_DIM // 2, dtype=F32))
    ang_r = pos[:, None] * inv_r[None, :]
    cos_r = jnp.concatenate([jnp.cos(ang_r)] * 2, axis=1)
    sin_r = jnp.concatenate([-jnp.sin(ang_r), jnp.sin(ang_r)], axis=1)
    return cos_a, sin_a, sin_b, cos_r, sin_r


def _dup_kv_columns(w):
    heads = [w[:, h * ATTN_HEAD_DIM:(h + 1) * ATTN_HEAD_DIM] for h in range(N_KV_HEADS)]
    return jnp.concatenate([heads[0], heads[0], heads[1], heads[1]], axis=1)


def _layer(x, p):
    b, s, _ = x.shape
    n = b * s
    tables = _rope_tables(s)
    h, q, k, v, rq, rk, rv, rg = _ffn_proj(
        x.reshape(n, D_MODEL), s, p["g1"], p["wg1"], p["wu1"], p["wd1"], p["gm"], p["win"], tables)
    seq = lambda a: a.reshape(b, s, a.shape[-1])
    attn = _attn(seq(q), seq(k), seq(v), p["sink"], p["ga"])
    sb = _ret_state(seq(rk), seq(rv), p["ldb"])
    ret = _ret_out(seq(rq), seq(rk), seq(rv), seq(rg), sb, p["ldf"], p["ldb"])
    y = _out_ffn(h, attn.reshape(n, ATTN_WIDTH), ret.reshape(n, RET_WIDTH),
                 p["woa"], p["wor"], p["g2"], p["wg2"], p["wu2"], p["wd2"], p["gf"])
    return y.reshape(b, s, D_MODEL)


def kernel(x_prompt, x_sample, ffn1_norm, ffn1_w_gate, ffn1_w_up, ffn1_w_down, mix_norm, w_in, attn_sink, attn_out_norm, ret_log_decay_fwd, ret_log_decay_bwd, w_out, ffn2_norm, ffn2_w_gate, ffn2_w_up, ffn2_w_down, final_norm):
    assert ffn1_norm.shape[0] == 1
    o1 = ATTN_WIDTH
    o2 = o1 + N_KV_HEADS * ATTN_HEAD_DIM
    o3 = o2 + N_KV_HEADS * ATTN_HEAD_DIM
    wi = w_in[0]
    win = jnp.concatenate([wi[:, :o1], _dup_kv_columns(wi[:, o1:o2]), _dup_kv_columns(wi[:, o2:o3]),
                           wi[:, o3:]], axis=1).astype(BF16)
    wo = w_out[0].astype(BF16)
    p = dict(
        g1=ffn1_norm[0][None, :], wg1=ffn1_w_gate[0].astype(BF16), wu1=ffn1_w_up[0].astype(BF16),
        wd1=ffn1_w_down[0].astype(BF16), gm=mix_norm[0][None, :], win=win,
        sink=attn_sink[0], ga=attn_out_norm[0][None, :],
        ldf=ret_log_decay_fwd[0], ldb=ret_log_decay_bwd[0],
        woa=wo[:ATTN_WIDTH], wor=wo[ATTN_WIDTH:],
        g2=ffn2_norm[0][None, :], wg2=ffn2_w_gate[0].astype(BF16), wu2=ffn2_w_up[0].astype(BF16),
        wd2=ffn2_w_down[0].astype(BF16), gf=final_norm[None, :],
    )
    return (_layer(x_prompt, p), _layer(x_sample, p))
```

```python
import functools

import jax
import jax.numpy as jnp
from jax import lax
from jax.experimental import pallas as pl
from jax.experimental.pallas import tpu as pltpu

D_MODEL = 1024
D_FF = 2816
N_ATTN_HEADS = 8
N_KV_HEADS = 2
ATTN_HEAD_DIM = 64
ATTN_WIDTH = N_ATTN_HEADS * ATTN_HEAD_DIM
WINDOW = 128
ATTN_BLOCK = 128
ROT_DIM = 16
ROPE_THETA = 500000.0
RET_HEAD_DIM = 128
N_RET_HEADS = 4
RET_WIDTH = N_RET_HEADS * RET_HEAD_DIM
RET_CHUNK = 128
RET_ROT_THETA = 10000.0
EPS = 1e-6
NEG_BIG = -1e30
LOG2_E = 1.4426950408889634
ATTN_Q_SCALE = ATTN_HEAD_DIM ** -0.5 * LOG2_E

LANES = 128
KV_DUP_WIDTH = 2 * LANES
FF_CHUNK = 256
PROJ_CHUNK = 512
TOKEN_TILE = 512
SEQ_TILE = 512
RET_STATE_TILE = 2048
VMEM_LIMIT = 56 * 1024 * 1024

BF16 = jnp.bfloat16
F32 = jnp.float32


def _dot(a, b):
    return jnp.dot(a, b, preferred_element_type=F32)


def _dot_nt(a, b):
    return lax.dot_general(a, b, (((1,), (1,)), ((), ())), preferred_element_type=F32)


def _dot_tn(a, b):
    return lax.dot_general(a, b, (((0,), (0,)), ((), ())), preferred_element_type=F32)


def _rmsnorm(x, g):
    return x * lax.rsqrt(jnp.mean(x * x, axis=-1, keepdims=True) + EPS) * g


def _silu(x):
    return x * (1.0 / (1.0 + jnp.exp(-x)))


def _swiglu(xn, wg_ref, wu_ref, wd_ref, act_ref):
    for c in range(D_FF // FF_CHUNK):
        cols = slice(c * FF_CHUNK, (c + 1) * FF_CHUNK)
        g = _dot(xn, wg_ref[:, cols])
        u = _dot(xn, wu_ref[:, cols])
        act_ref[:, cols] = (_silu(g) * u).astype(BF16)
    return _dot(act_ref[...], wd_ref[...])


def _ffn_proj_kernel(ldf_ref, ldb_ref, x_ref, g1_ref, wg_ref, wu_ref, wd_ref, gm_ref, win_ref,
                     cosa_ref, sina_ref, sinb_ref, cosr_ref, sinr_ref,
                     h_ref, q_ref, k_ref, v_ref, rq_ref, rk_ref, rkf_ref, rkb_ref, rv_ref, gate_ref,
                     act_ref, decay_ref):
    @pl.when(pl.program_id(0) == 0)
    def _():
        shape = decay_ref.shape[1:]
        j = (lax.broadcasted_iota(jnp.int32, shape, 0) % RET_CHUNK).astype(F32)
        decay_ref[0] = jnp.exp(_head_lane_values(ldf_ref) * (float(RET_CHUNK) - 1.0 - j))
        decay_ref[1] = jnp.exp(_head_lane_values(ldb_ref) * j)

    x = x_ref[...]
    xn = _rmsnorm(x, g1_ref[...]).astype(BF16)
    h = x + 0.5 * _swiglu(xn, wg_ref, wu_ref, wd_ref, act_ref)
    h_ref[...] = h
    un = _rmsnorm(h, gm_ref[...]).astype(BF16)

    cosa, sina, sinb = cosa_ref[...], sina_ref[...], sinb_ref[...]
    cosr, sinr = cosr_ref[...], sinr_ref[...]

    def rope_a(xb):
        return (xb * cosa + pltpu.roll(xb, LANES - ROT_DIM // 2, 1) * sina
                + pltpu.roll(xb, ROT_DIM // 2, 1) * sinb)

    def rope_r(xb):
        return xb * cosr + pltpu.roll(xb, RET_HEAD_DIM // 2, 1) * sinr

    def lane_blocks(p):
        return [p[:, c * LANES:(c + 1) * LANES] for c in range(p.shape[1] // LANES)]

    def proj(section):
        return _dot(un, win_ref[:, section * PROJ_CHUNK:(section + 1) * PROJ_CHUNK])

    for c, blk in enumerate(lane_blocks(proj(3))):
        cols = slice(c * LANES, (c + 1) * LANES)
        rk = rope_r(blk) * (RET_HEAD_DIM ** -0.5)
        rk_ref[:, cols] = rk.astype(BF16)
        rkf_ref[:, cols] = (rk * decay_ref[0, :, cols]).astype(BF16)
        rkb_ref[:, cols] = (rk * decay_ref[1, :, cols]).astype(BF16)
    gate_ref[...] = _silu(proj(5)).astype(BF16)
    for c, blk in enumerate(lane_blocks(proj(0))):
        q_ref[:, c * LANES:(c + 1) * LANES] = (rope_a(blk) * ATTN_Q_SCALE).astype(BF16)
    for c, blk in enumerate(lane_blocks(proj(2))):
        rq_ref[:, c * LANES:(c + 1) * LANES] = rope_r(blk).astype(BF16)
    kv = lane_blocks(proj(1))
    for c, blk in enumerate(kv[:KV_DUP_WIDTH // LANES]):
        k_ref[:, c * LANES:(c + 1) * LANES] = rope_a(blk).astype(BF16)
    for c, blk in enumerate(kv[KV_DUP_WIDTH // LANES:]):
        v_ref[:, c * LANES:(c + 1) * LANES] = blk.astype(BF16)
    rv_ref[...] = proj(4).astype(BF16)


def _const_spec(shape):
    return pl.BlockSpec(shape, lambda *_: (0,) * len(shape), pipeline_mode=pl.Buffered(1))


def _ffn_proj(x2d, seq_len, g1, wg, wu, wd, gm, win, tables, ld_f, ld_b):
    n = x2d.shape[0]
    tm = min(TOKEN_TILE, seq_len)
    assert tm % RET_CHUNK == 0
    tiles_per_seq = seq_len // tm
    row = lambda i: (i, 0)
    tab = lambda i: (i % tiles_per_seq, 0)
    win_cols = win.shape[1]
    out_widths = (ATTN_WIDTH, KV_DUP_WIDTH, KV_DUP_WIDTH) + (RET_WIDTH,) * 6
    return pl.pallas_call(
        _ffn_proj_kernel,
        grid=(n // tm,),
        in_specs=[
            pl.BlockSpec(memory_space=pltpu.SMEM),
            pl.BlockSpec(memory_space=pltpu.SMEM),
            pl.BlockSpec((tm, D_MODEL), row),
            _const_spec((1, D_MODEL)),
            _const_spec((D_MODEL, D_FF)),
            _const_spec((D_MODEL, D_FF)),
            _const_spec((D_FF, D_MODEL)),
            _const_spec((1, D_MODEL)),
            _const_spec((D_MODEL, win_cols)),
        ] + [pl.BlockSpec((tm, LANES), tab)] * 5,
        out_specs=[pl.BlockSpec((tm, D_MODEL), row)] + [pl.BlockSpec((tm, w), row) for w in out_widths],
        out_shape=[jax.ShapeDtypeStruct((n, D_MODEL), F32)]
                  + [jax.ShapeDtypeStruct((n, w), BF16) for w in out_widths],
        scratch_shapes=[pltpu.VMEM((tm, D_FF), BF16),
                        pltpu.VMEM((2, tm, RET_WIDTH), F32)],
        compiler_params=pltpu.CompilerParams(dimension_semantics=("arbitrary",),
                                             vmem_limit_bytes=VMEM_LIMIT),
        name="ffn_proj",
    )(ld_f, ld_b, x2d, g1, wg, wu, wd, gm, win, *tables)


def _attn_kernel(sink_ref, q_ref, kp_ref, km_ref, kn_ref, vp_ref, vm_ref, vn_ref, gain_ref,
                 o_ref, klo_ref, khi_ref, vw_ref, bias_ref, s_ref, p_ref, r_ref, acc_ref):
    t = pl.program_id(1)
    nt = pl.num_programs(1)
    ts = q_ref.shape[1]
    nblk = ts // ATTN_BLOCK
    wk = 3 * ATTN_BLOCK

    @pl.when(t == 0)
    def _():
        qi = lax.broadcasted_iota(jnp.int32, (ATTN_BLOCK, wk), 0)
        kj = lax.broadcasted_iota(jnp.int32, (ATTN_BLOCK, wk), 1)
        band = (kj >= qi) & (kj <= qi + 2 * WINDOW)
        bias_ref[0] = jnp.where(band, 0.0, NEG_BIG)
        bias_ref[1] = jnp.where(band & (kj >= ATTN_BLOCK), 0.0, NEG_BIG)
        bias_ref[2] = jnp.where(band & (kj < 2 * ATTN_BLOCK), 0.0, NEG_BIG)

    lo_lanes = (lax.broadcasted_iota(jnp.int32, (1, KV_DUP_WIDTH), 1) % LANES) < ATTN_HEAD_DIM
    for dst, src in ((slice(0, ATTN_BLOCK), kp_ref), (slice(ATTN_BLOCK, ATTN_BLOCK + ts), km_ref),
                     (slice(ATTN_BLOCK + ts, 2 * ATTN_BLOCK + ts), kn_ref)):
        kblk = src[0]
        klo_ref[dst, :] = jnp.where(lo_lanes, kblk, jnp.zeros_like(kblk))
        khi_ref[dst, :] = jnp.where(lo_lanes, jnp.zeros_like(kblk), kblk)
    vw_ref[0:ATTN_BLOCK, :] = vp_ref[0]
    vw_ref[ATTN_BLOCK:ATTN_BLOCK + ts, :] = vm_ref[0]
    vw_ref[ATTN_BLOCK + ts:, :] = vn_ref[0]

    out_lo = lax.broadcasted_iota(jnp.int32, (ATTN_BLOCK, LANES), 1) < ATTN_HEAD_DIM
    first_idx = jnp.where(t == 0, 1, 0)
    last_idx = jnp.where(t == nt - 1, 2, 0)

    units = [(j, g, pair, half) for j in range(nblk) for g in range(N_KV_HEADS)
             for pair in range(2) for half in range(2)]

    def rows_of(j):
        return slice(j * ATTN_BLOCK, (j + 1) * ATTN_BLOCK)

    def win_rows(j):
        return slice(j * ATTN_BLOCK, j * ATTN_BLOCK + wk)

    def pair_cols(g, pair):
        return slice((2 * g + pair) * LANES, (2 * g + pair + 1) * LANES)

    for u, (j, g, pair, half) in enumerate(units):
        k_ref = khi_ref if half else klo_ref
        bias_idx = first_idx if j == 0 else (last_idx if j == nblk - 1 else 0)
        s = _dot_nt(q_ref[0, rows_of(j), pair_cols(g, pair)],
                    k_ref[win_rows(j), g * LANES:(g + 1) * LANES])
        for blk in range(3):
            cols = slice(blk * ATTN_BLOCK, (blk + 1) * ATTN_BLOCK)
            s_ref[u, :, cols] = s[:, cols] if blk == 1 else s[:, cols] + bias_ref[bias_idx, :, cols]

    for u, (j, g, pair, half) in enumerate(units):
        s = s_ref[u]
        sink = sink_ref[4 * g + 2 * pair + half] * LOG2_E
        m = jnp.maximum(jnp.max(s, axis=-1, keepdims=True), sink)
        p = jnp.exp2(s - m)
        denom = jnp.sum(p, axis=-1, keepdims=True) + jnp.exp2(sink - m)
        p_ref[u] = p.astype(BF16)
        r_ref[u] = jnp.broadcast_to(1.0 / denom, (ATTN_BLOCK, LANES))

    pending = []
    for u, (j, g, pair, half) in enumerate(units):
        out = _dot(p_ref[u], vw_ref[win_rows(j), g * LANES:(g + 1) * LANES]) * r_ref[u]
        if half == 0:
            pending.append(out)
        else:
            acc_ref[rows_of(j), pair_cols(g, pair)] = jnp.where(out_lo, pending.pop(), out)

    o_ref[0] = _rmsnorm(acc_ref[...], gain_ref[...]).astype(BF16)


def _attn(q, k, v, sink, gain):
    b, s, _ = q.shape
    ts = min(SEQ_TILE, s)
    nt = s // ts
    bps = ts // ATTN_BLOCK
    nb = s // ATTN_BLOCK
    main = lambda i, t: (i, t, 0)
    prev = lambda i, t: (i, jnp.maximum(t * bps - 1, 0), 0)
    nxt = lambda i, t: (i, jnp.minimum((t + 1) * bps, nb - 1), 0)
    kv_specs = [pl.BlockSpec((1, ATTN_BLOCK, KV_DUP_WIDTH), prev),
                pl.BlockSpec((1, ts, KV_DUP_WIDTH), main),
                pl.BlockSpec((1, ATTN_BLOCK, KV_DUP_WIDTH), nxt)]
    return pl.pallas_call(
        _attn_kernel,
        grid=(b, nt),
        in_specs=[pl.BlockSpec(memory_space=pltpu.SMEM),
                  pl.BlockSpec((1, ts, ATTN_WIDTH), main)] + kv_specs + kv_specs
                 + [pl.BlockSpec((1, ATTN_WIDTH), lambda i, t: (0, 0))],
        out_specs=pl.BlockSpec((1, ts, ATTN_WIDTH), main),
        out_shape=jax.ShapeDtypeStruct((b, s, ATTN_WIDTH), BF16),
        scratch_shapes=[pltpu.VMEM((ts + 2 * ATTN_BLOCK, KV_DUP_WIDTH), BF16),
                        pltpu.VMEM((ts + 2 * ATTN_BLOCK, KV_DUP_WIDTH), BF16),
                        pltpu.VMEM((ts + 2 * ATTN_BLOCK, KV_DUP_WIDTH), BF16),
                        pltpu.VMEM((3, ATTN_BLOCK, 3 * ATTN_BLOCK), F32),
                        pltpu.VMEM((bps * N_ATTN_HEADS, ATTN_BLOCK, 3 * ATTN_BLOCK), F32),
                        pltpu.VMEM((bps * N_ATTN_HEADS, ATTN_BLOCK, 3 * ATTN_BLOCK), BF16),
                        pltpu.VMEM((bps * N_ATTN_HEADS, ATTN_BLOCK, LANES), F32),
                        pltpu.VMEM((ts, ATTN_WIDTH), F32)],
        compiler_params=pltpu.CompilerParams(dimension_semantics=("parallel", "arbitrary"),
                                             vmem_limit_bytes=VMEM_LIMIT),
        name="attn",
    )(sink, q, k, k, k, v, v, v, gain)


def _head_lane_values(ld_ref):
    lane = lax.broadcasted_iota(jnp.int32, (1, RET_WIDTH), 1)
    ld = jnp.zeros((1, RET_WIDTH), F32)
    for h in range(N_RET_HEADS):
        ld = jnp.where(lane >= h * RET_HEAD_DIM, ld_ref[h], ld)
    return ld


def _chunk_decay(ld_ref):
    return jnp.exp(_head_lane_values(ld_ref) * float(RET_CHUNK))


def _head_cols(h):
    return slice(h * RET_HEAD_DIM, (h + 1) * RET_HEAD_DIM)


def _chunk_rows(c):
    return slice(c * RET_CHUNK, (c + 1) * RET_CHUNK)


def _ret_state_kernel(ldb_ref, rkb_ref, rv_ref, sb_ref, state_ref, kv_ref):
    t = pl.program_id(1)
    nchunk = rkb_ref.shape[1] // RET_CHUNK

    @pl.when(t == 0)
    def _():
        state_ref[...] = jnp.zeros_like(state_ref)

    for c in range(nchunk):
        for h in range(N_RET_HEADS):
            kv_ref[c, :, _head_cols(h)] = _dot_tn(rkb_ref[0, _chunk_rows(c), _head_cols(h)],
                                                  rv_ref[0, _chunk_rows(c), _head_cols(h)])

    chunk_decay = _chunk_decay(ldb_ref)
    state = state_ref[...]
    for c in reversed(range(nchunk)):
        sb_ref[0, c] = state.astype(BF16)
        state = state * chunk_decay + kv_ref[c]
    state_ref[...] = state


def _ret_state(rkb, rv, ld_b):
    b, s, _ = rkb.shape
    ts = min(RET_STATE_TILE, s)
    nt = s // ts
    cpt = ts // RET_CHUNK
    rev = lambda i, t: (i, nt - 1 - t, 0)
    return pl.pallas_call(
        _ret_state_kernel,
        grid=(b, nt),
        in_specs=[pl.BlockSpec(memory_space=pltpu.SMEM),
                  pl.BlockSpec((1, ts, RET_WIDTH), rev),
                  pl.BlockSpec((1, ts, RET_WIDTH), rev)],
        out_specs=pl.BlockSpec((1, cpt, RET_HEAD_DIM, RET_WIDTH), lambda i, t: (i, nt - 1 - t, 0, 0)),
        out_shape=jax.ShapeDtypeStruct((b, s // RET_CHUNK, RET_HEAD_DIM, RET_WIDTH), BF16),
        scratch_shapes=[pltpu.VMEM((RET_HEAD_DIM, RET_WIDTH), F32),
                        pltpu.VMEM((cpt, RET_HEAD_DIM, RET_WIDTH), F32)],
        compiler_params=pltpu.CompilerParams(dimension_semantics=("parallel", "arbitrary"),
                                             vmem_limit_bytes=VMEM_LIMIT),
        name="ret_state",
    )(ld_b, rkb, rv)


def _ret_out_kernel(ldf_ref, ldb_ref, rq_ref, rk_ref, rkf_ref, rv_ref, gate_ref, sb_ref, o_ref,
                    state_ref, tab_ref, inner_ref, kv_ref, cs_ref):
    t = pl.program_id(1)
    nchunk = rq_ref.shape[1] // RET_CHUNK
    MASK, Q_FWD, Q_BWD = range(3)

    @pl.when(t == 0)
    def _():
        state_ref[...] = jnp.zeros_like(state_ref)
        shape = (RET_CHUNK, RET_HEAD_DIM)
        row_i = lax.broadcasted_iota(jnp.int32, shape, 0)
        col_i = lax.broadcasted_iota(jnp.int32, shape, 1)
        row = row_i.astype(F32)
        diff = (row_i - col_i).astype(F32)
        for h in range(N_RET_HEADS):
            d_fwd = jnp.where(diff >= 0, jnp.exp(ldf_ref[h] * jnp.maximum(diff, 0.0)), 0.0)
            d_bwd = jnp.where(diff <= 0, jnp.exp(ldb_ref[h] * jnp.maximum(-diff, 0.0)), 0.0)
            tab_ref[MASK, h] = d_fwd + d_bwd
            tab_ref[Q_FWD, h] = jnp.exp(ldf_ref[h] * (row + 1.0))
            tab_ref[Q_BWD, h] = jnp.exp(ldb_ref[h] * (float(RET_CHUNK) - row))

    units = [(c, h) for c in range(nchunk) for h in range(N_RET_HEADS)]

    for u, (c, h) in enumerate(units):
        q = rq_ref[0, _chunk_rows(c), _head_cols(h)]
        k = rk_ref[0, _chunk_rows(c), _head_cols(h)]
        inner_ref[u] = (_dot_nt(q, k) * tab_ref[MASK, h]).astype(BF16)
        kv_ref[c, :, _head_cols(h)] = _dot_tn(rkf_ref[0, _chunk_rows(c), _head_cols(h)],
                                              rv_ref[0, _chunk_rows(c), _head_cols(h)])

    chunk_decay = _chunk_decay(ldf_ref)
    state = state_ref[...]
    for c in range(nchunk):
        fwd = state.astype(BF16)
        for h in range(N_RET_HEADS):
            base = 2 * h * RET_HEAD_DIM
            cs_ref[c, :, base:base + RET_HEAD_DIM] = fwd[:, _head_cols(h)]
            cs_ref[c, :, base + RET_HEAD_DIM:base + 2 * RET_HEAD_DIM] = sb_ref[0, c, :, _head_cols(h)]
        state = state * chunk_decay + kv_ref[c]
    state_ref[...] = state

    for u, (c, h) in enumerate(units):
        q = rq_ref[0, _chunk_rows(c), _head_cols(h)]
        o = _dot(inner_ref[u], rv_ref[0, _chunk_rows(c), _head_cols(h)])
        cross = _dot(q, cs_ref[c, :, 2 * h * RET_HEAD_DIM:2 * (h + 1) * RET_HEAD_DIM])
        o += tab_ref[Q_FWD, h] * cross[:, :RET_HEAD_DIM] + tab_ref[Q_BWD, h] * cross[:, RET_HEAD_DIM:]
        mu = jnp.mean(o, axis=-1, keepdims=True)
        var = jnp.mean(jnp.square(o - mu), axis=-1, keepdims=True)
        on = (o - mu) * lax.rsqrt(var + EPS)
        gate = gate_ref[0, _chunk_rows(c), _head_cols(h)].astype(F32)
        o_ref[0, _chunk_rows(c), _head_cols(h)] = (gate * on).astype(BF16)


def _ret_out(rq, rk, rkf, rv, gate, sb, ld_f, ld_b):
    b, s, _ = rq.shape
    ts = min(SEQ_TILE, s)
    nt = s // ts
    cpt = ts // RET_CHUNK
    main = lambda i, t: (i, t, 0)
    seq_spec = pl.BlockSpec((1, ts, RET_WIDTH), main)
    return pl.pallas_call(
        _ret_out_kernel,
        grid=(b, nt),
        in_specs=[pl.BlockSpec(memory_space=pltpu.SMEM), pl.BlockSpec(memory_space=pltpu.SMEM),
                  seq_spec, seq_spec, seq_spec, seq_spec, seq_spec,
                  pl.BlockSpec((1, cpt, RET_HEAD_DIM, RET_WIDTH), lambda i, t: (i, t, 0, 0))],
        out_specs=seq_spec,
        out_shape=jax.ShapeDtypeStruct((b, s, RET_WIDTH), BF16),
        scratch_shapes=[pltpu.VMEM((RET_HEAD_DIM, RET_WIDTH), F32),
                        pltpu.VMEM((3, N_RET_HEADS, RET_CHUNK, RET_HEAD_DIM), F32),
                        pltpu.VMEM((cpt * N_RET_HEADS, RET_CHUNK, RET_CHUNK), BF16),
                        pltpu.VMEM((cpt, RET_HEAD_DIM, RET_WIDTH), F32),
                        pltpu.VMEM((cpt, RET_HEAD_DIM, 2 * RET_WIDTH), BF16)],
        compiler_params=pltpu.CompilerParams(dimension_semantics=("parallel", "arbitrary"),
                                             vmem_limit_bytes=VMEM_LIMIT),
        name="ret_out",
    )(ld_f, ld_b, rq, rk, rkf, rv, gate, sb)


def _out_ffn_kernel(h_ref, a_ref, r_ref, woa_ref, wor_ref, g2_ref, wg_ref, wu_ref, wd_ref, gf_ref,
                    y_ref, act_ref):
    h = h_ref[...] + _dot(a_ref[...], woa_ref[...]) + _dot(r_ref[...], wor_ref[...])
    hn = _rmsnorm(h, g2_ref[...]).astype(BF16)
    h = h + 0.5 * _swiglu(hn, wg_ref, wu_ref, wd_ref, act_ref)
    y_ref[...] = _rmsnorm(h, gf_ref[...])


def _out_ffn(h, a, r, woa, wor, g2, wg, wu, wd, gf):
    n = h.shape[0]
    tm = min(TOKEN_TILE, n)
    row = lambda i: (i, 0)
    return pl.pallas_call(
        _out_ffn_kernel,
        grid=(n // tm,),
        in_specs=[
            pl.BlockSpec((tm, D_MODEL), row),
            pl.BlockSpec((tm, ATTN_WIDTH), row),
            pl.BlockSpec((tm, RET_WIDTH), row),
            _const_spec((ATTN_WIDTH, D_MODEL)),
            _const_spec((RET_WIDTH, D_MODEL)),
            _const_spec((1, D_MODEL)),
            _const_spec((D_MODEL, D_FF)),
            _const_spec((D_MODEL, D_FF)),
            _const_spec((D_FF, D_MODEL)),
            _const_spec((1, D_MODEL)),
        ],
        out_specs=pl.BlockSpec((tm, D_MODEL), row),
        out_shape=jax.ShapeDtypeStruct((n, D_MODEL), F32),
        scratch_shapes=[pltpu.VMEM((tm, D_FF), BF16)],
        compiler_params=pltpu.CompilerParams(dimension_semantics=("parallel",),
                                             vmem_limit_bytes=VMEM_LIMIT),
        name="out_ffn",
    )(h, a, r, woa, wor, g2, wg, wu, wd, gf)


def _rope_tables(seq_len):
    pos = jnp.arange(seq_len, dtype=F32)[:, None]
    half = ROT_DIM // 2
    lane = jnp.arange(LANES) % ATTN_HEAD_DIM
    inv_a = ROPE_THETA ** (-jnp.arange(0, ROT_DIM, 2, dtype=F32) / ROT_DIM)
    freq_a = jnp.where(lane < ROT_DIM, inv_a[lane % half], 0.0)
    ang_a = pos * freq_a[None, :]
    cos_a, sin = jnp.cos(ang_a), jnp.sin(ang_a)
    sin_a = sin * jnp.where(lane < half, -1.0, 0.0)[None, :]
    sin_b = sin * jnp.where((lane >= half) & (lane < ROT_DIM), 1.0, 0.0)[None, :]
    inv_r = RET_ROT_THETA ** (-jnp.linspace(0.0, 1.0, RET_HEAD_DIM // 2, dtype=F32))
    ang_r = pos * jnp.concatenate([inv_r, inv_r])[None, :]
    sign_r = jnp.where(jnp.arange(LANES) < RET_HEAD_DIM // 2, -1.0, 1.0)
    return cos_a, sin_a, sin_b, jnp.cos(ang_r), jnp.sin(ang_r) * sign_r[None, :]


def _dup_kv_columns(w):
    heads = [w[:, h * ATTN_HEAD_DIM:(h + 1) * ATTN_HEAD_DIM] for h in range(N_KV_HEADS)]
    return jnp.concatenate([heads[0], heads[0], heads[1], heads[1]], axis=1)


def _layer(x, p, tables):
    b, s, _ = x.shape
    n = b * s
    tables = tuple(t[:s] for t in tables)
    h, q, k, v, rq, rk, rkf, rkb, rv, gate = _ffn_proj(
        x.reshape(n, D_MODEL), s, p["g1"], p["wg1"], p["wu1"], p["wd1"], p["gm"], p["win"], tables,
        p["ldf"], p["ldb"])
    seq = lambda a: a.reshape(b, s, a.shape[-1])
    attn = _attn(seq(q), seq(k), seq(v), p["sink"], p["ga"])
    sb = _ret_state(seq(rkb), seq(rv), p["ldb"])
    ret = _ret_out(seq(rq), seq(rk), seq(rkf), seq(rv), seq(gate), sb, p["ldf"], p["ldb"])
    y = _out_ffn(h, attn.reshape(n, ATTN_WIDTH), ret.reshape(n, RET_WIDTH),
                 p["woa"], p["wor"], p["g2"], p["wg2"], p["wu2"], p["wd2"], p["gf"])
    return y.reshape(b, s, D_MODEL)


def kernel(x_prompt, x_sample, ffn1_norm, ffn1_w_gate, ffn1_w_up, ffn1_w_down, mix_norm, w_in, attn_sink, attn_out_norm, ret_log_decay_fwd, ret_log_decay_bwd, w_out, ffn2_norm, ffn2_w_gate, ffn2_w_up, ffn2_w_down, final_norm):
    assert ffn1_norm.shape[0] == 1
    o1 = ATTN_WIDTH
    o2 = o1 + N_KV_HEADS * ATTN_HEAD_DIM
    o3 = o2 + N_KV_HEADS * ATTN_HEAD_DIM
    wi = w_in[0]
    win = jnp.concatenate([wi[:, :o1], _dup_kv_columns(wi[:, o1:o2]), _dup_kv_columns(wi[:, o2:o3]),
                           wi[:, o3:]], axis=1).astype(BF16)
    wo = w_out[0].astype(BF16)
    p = dict(
        g1=ffn1_norm[0][None, :], wg1=ffn1_w_gate[0].astype(BF16), wu1=ffn1_w_up[0].astype(BF16),
        wd1=ffn1_w_down[0].astype(BF16), gm=mix_norm[0][None, :], win=win,
        sink=attn_sink[0], ga=attn_out_norm[0][None, :],
        ldf=ret_log_decay_fwd[0], ldb=ret_log_decay_bwd[0],
        woa=wo[:ATTN_WIDTH], wor=wo[ATTN_WIDTH:],
        g2=ffn2_norm[0][None, :], wg2=ffn2_w_gate[0].astype(BF16), wu2=ffn2_w_up[0].astype(BF16),
        wd2=ffn2_w_down[0].astype(BF16), gf=final_norm[None, :],
    )
    tables = _rope_tables(max(x_prompt.shape[1], x_sample.shape[1]))
    return (_layer(x_prompt, p, tables), _layer(x_sample, p, tables))
```

```python
import functools

import jax
import jax.numpy as jnp
from jax import lax
from jax.experimental import pallas as pl
from jax.experimental.pallas import tpu as pltpu

D_MODEL = 1024
D_FF = 2816
N_ATTN_HEADS = 8
N_KV_HEADS = 2
ATTN_HEAD_DIM = 64
ATTN_WIDTH = N_ATTN_HEADS * ATTN_HEAD_DIM
WINDOW = 128
ATTN_BLOCK = 128
ROT_DIM = 16
ROPE_THETA = 500000.0
RET_HEAD_DIM = 128
N_RET_HEADS = 4
RET_WIDTH = N_RET_HEADS * RET_HEAD_DIM
RET_CHUNK = 128
RET_ROT_THETA = 10000.0
EPS = 1e-6
NEG_BIG = -1e30
LOG2_E = 1.4426950408889634
ATTN_Q_SCALE = ATTN_HEAD_DIM ** -0.5 * LOG2_E

LANES = 128
KV_DUP_WIDTH = 2 * LANES
FF_CHUNK = 256
PROJ_CHUNK = 512
TOKEN_TILE = 512
SEQ_TILE = 1024
RET_STATE_TILE = 2048
ROPE_BLOCK = 128
VMEM_LIMIT = 56 * 1024 * 1024

BF16 = jnp.bfloat16
F32 = jnp.float32


def _dot(a, b):
    return jnp.dot(a, b, preferred_element_type=F32)


def _dot_nt(a, b):
    return lax.dot_general(a, b, (((1,), (1,)), ((), ())), preferred_element_type=F32)


def _dot_tn(a, b):
    return lax.dot_general(a, b, (((0,), (0,)), ((), ())), preferred_element_type=F32)


def _rmsnorm(x, g):
    return x * lax.rsqrt(jnp.mean(x * x, axis=-1, keepdims=True) + EPS) * g


def _silu(x):
    return x * (1.0 / (1.0 + jnp.exp(-x)))


def _swiglu(xn, wg_ref, wu_ref, wd_ref, act_ref):
    for c in range(D_FF // FF_CHUNK):
        cols = slice(c * FF_CHUNK, (c + 1) * FF_CHUNK)
        g = _dot(xn, wg_ref[:, cols])
        u = _dot(xn, wu_ref[:, cols])
        act_ref[:, cols] = (_silu(g) * u).astype(BF16)
    return _dot(act_ref[...], wd_ref[...])


def _ffn_proj_kernel(ldf_ref, ldb_ref, x_ref, g1_ref, wg_ref, wu_ref, wd_ref, gm_ref,
                     wq_ref, wkv_ref, wret_ref,
                     cosa_ref, sina_ref, sinb_ref, cosr_ref, sinr_ref,
                     h_ref, q_ref, k_ref, v_ref, rq_ref, rk_ref, rkf_ref, rkb_ref, rv_ref, gate_ref,
                     act_ref, decay_ref):
    @pl.when(pl.program_id(0) == 0)
    def _():
        shape = decay_ref.shape[1:]
        j = (lax.broadcasted_iota(jnp.int32, shape, 0) % RET_CHUNK).astype(F32)
        decay_ref[0] = jnp.exp(_head_lane_values(ldf_ref) * (float(RET_CHUNK) - 1.0 - j))
        decay_ref[1] = jnp.exp(_head_lane_values(ldb_ref) * j)

    x = x_ref[...]
    xn = _rmsnorm(x, g1_ref[...]).astype(BF16)
    h = x + 0.5 * _swiglu(xn, wg_ref, wu_ref, wd_ref, act_ref)
    h_ref[...] = h
    un = _rmsnorm(h, gm_ref[...]).astype(BF16)

    cosa, sina, sinb = cosa_ref[...], sina_ref[...], sinb_ref[...]
    cosr, sinr = cosr_ref[...], sinr_ref[...]

    def rope_a(xb):
        return (xb * cosa + pltpu.roll(xb, LANES - ROT_DIM // 2, 1) * sina
                + pltpu.roll(xb, ROT_DIM // 2, 1) * sinb)

    def rope_r(xb):
        return xb * cosr + pltpu.roll(xb, RET_HEAD_DIM // 2, 1) * sinr

    def lane_blocks(p):
        return [p[:, c * LANES:(c + 1) * LANES] for c in range(p.shape[1] // LANES)]

    def proj(section):
        if section == 0:
            return _dot(un, wq_ref[...])
        if section == 1:
            return _dot(un, wkv_ref[...])
        return _dot(un, wret_ref[:, (section - 2) * PROJ_CHUNK:(section - 1) * PROJ_CHUNK])

    for c, blk in enumerate(lane_blocks(proj(3))):
        cols = slice(c * LANES, (c + 1) * LANES)
        rk = rope_r(blk) * (RET_HEAD_DIM ** -0.5)
        rk_ref[:, cols] = rk.astype(BF16)
        rkf_ref[:, cols] = (rk * decay_ref[0, :, cols]).astype(BF16)
        rkb_ref[:, cols] = (rk * decay_ref[1, :, cols]).astype(BF16)
    gate_ref[...] = _silu(proj(5)).astype(BF16)
    for c, blk in enumerate(lane_blocks(proj(0))):
        q_ref[:, c * LANES:(c + 1) * LANES] = (rope_a(blk) * ATTN_Q_SCALE).astype(BF16)
    for c, blk in enumerate(lane_blocks(proj(2))):
        rq_ref[:, c * LANES:(c + 1) * LANES] = rope_r(blk).astype(BF16)
    kv = lane_blocks(proj(1))
    for c, blk in enumerate(kv[:KV_DUP_WIDTH // LANES]):
        k_ref[:, c * LANES:(c + 1) * LANES] = rope_a(blk).astype(BF16)
    for c, blk in enumerate(kv[KV_DUP_WIDTH // LANES:]):
        v_ref[:, c * LANES:(c + 1) * LANES] = blk.astype(BF16)
    rv_ref[...] = proj(4).astype(BF16)


def _const_spec(shape):
    return pl.BlockSpec(shape, lambda *_: (0,) * len(shape), pipeline_mode=pl.Buffered(1))


def _ffn_proj(x2d, seq_len, g1, wg, wu, wd, gm, wq, wkv, wret, tables, ld_f, ld_b):
    n = x2d.shape[0]
    tm = min(TOKEN_TILE, seq_len)
    assert tm % RET_CHUNK == 0
    tiles_per_seq = seq_len // tm
    row = lambda i: (i, 0)
    tab = lambda i: (i % tiles_per_seq, 0)
    out_widths = (ATTN_WIDTH, KV_DUP_WIDTH, KV_DUP_WIDTH) + (RET_WIDTH,) * 6
    return pl.pallas_call(
        _ffn_proj_kernel,
        grid=(n // tm,),
        in_specs=[
            pl.BlockSpec(memory_space=pltpu.SMEM),
            pl.BlockSpec(memory_space=pltpu.SMEM),
            pl.BlockSpec((tm, D_MODEL), row),
            _const_spec((1, D_MODEL)),
            _const_spec((D_MODEL, D_FF)),
            _const_spec((D_MODEL, D_FF)),
            _const_spec((D_FF, D_MODEL)),
            _const_spec((1, D_MODEL)),
            _const_spec(wq.shape),
            _const_spec(wkv.shape),
            _const_spec(wret.shape),
        ] + [pl.BlockSpec((tm, LANES), tab)] * 5,
        out_specs=[pl.BlockSpec((tm, D_MODEL), row)] + [pl.BlockSpec((tm, w), row) for w in out_widths],
        out_shape=[jax.ShapeDtypeStruct((n, D_MODEL), F32)]
                  + [jax.ShapeDtypeStruct((n, w), BF16) for w in out_widths],
        scratch_shapes=[pltpu.VMEM((tm, D_FF), BF16),
                        pltpu.VMEM((2, tm, RET_WIDTH), F32)],
        compiler_params=pltpu.CompilerParams(dimension_semantics=("arbitrary",),
                                             vmem_limit_bytes=VMEM_LIMIT),
        name="ffn_proj",
    )(ld_f, ld_b, x2d, g1, wg, wu, wd, gm, wq, wkv, wret, *tables)


def _attn_kernel(sink_ref, q_ref, kp_ref, km_ref, kn_ref, vp_ref, vm_ref, vn_ref, gain_ref,
                 o_ref, klo_ref, khi_ref, vw_ref, bias_ref, s_ref, p_ref, r_ref, acc_ref):
    t = pl.program_id(1)
    nt = pl.num_programs(1)
    ts = q_ref.shape[1]
    nblk = ts // ATTN_BLOCK
    wk = 3 * ATTN_BLOCK

    @pl.when(t == 0)
    def _():
        qi = lax.broadcasted_iota(jnp.int32, (ATTN_BLOCK, wk), 0)
        kj = lax.broadcasted_iota(jnp.int32, (ATTN_BLOCK, wk), 1)
        band = (kj >= qi) & (kj <= qi + 2 * WINDOW)
        bias_ref[0] = jnp.where(band, 0.0, NEG_BIG)
        bias_ref[1] = jnp.where(band & (kj >= ATTN_BLOCK), 0.0, NEG_BIG)
        bias_ref[2] = jnp.where(band & (kj < 2 * ATTN_BLOCK), 0.0, NEG_BIG)

    lo_lanes = (lax.broadcasted_iota(jnp.int32, (1, KV_DUP_WIDTH), 1) % LANES) < ATTN_HEAD_DIM
    for dst, src in ((slice(0, ATTN_BLOCK), kp_ref), (slice(ATTN_BLOCK, ATTN_BLOCK + ts), km_ref),
                     (slice(ATTN_BLOCK + ts, 2 * ATTN_BLOCK + ts), kn_ref)):
        kblk = src[0]
        klo_ref[dst, :] = jnp.where(lo_lanes, kblk, jnp.zeros_like(kblk))
        khi_ref[dst, :] = jnp.where(lo_lanes, jnp.zeros_like(kblk), kblk)
    vw_ref[0:ATTN_BLOCK, :] = vp_ref[0]
    vw_ref[ATTN_BLOCK:ATTN_BLOCK + ts, :] = vm_ref[0]
    vw_ref[ATTN_BLOCK + ts:, :] = vn_ref[0]

    out_lo = lax.broadcasted_iota(jnp.int32, (ATTN_BLOCK, LANES), 1) < ATTN_HEAD_DIM
    first_idx = jnp.where(t == 0, 1, 0)
    last_idx = jnp.where(t == nt - 1, 2, 0)

    units = [(j, g, pair, half) for j in range(nblk) for g in range(N_KV_HEADS)
             for pair in range(2) for half in range(2)]

    def rows_of(j):
        return slice(j * ATTN_BLOCK, (j + 1) * ATTN_BLOCK)

    def win_rows(j):
        return slice(j * ATTN_BLOCK, j * ATTN_BLOCK + wk)

    def pair_cols(g, pair):
        return slice((2 * g + pair) * LANES, (2 * g + pair + 1) * LANES)

    for u, (j, g, pair, half) in enumerate(units):
        k_ref = khi_ref if half else klo_ref
        bias_idx = first_idx if j == 0 else (last_idx if j == nblk - 1 else 0)
        s = _dot_nt(q_ref[0, rows_of(j), pair_cols(g, pair)],
                    k_ref[win_rows(j), g * LANES:(g + 1) * LANES])
        for blk in range(3):
            cols = slice(blk * ATTN_BLOCK, (blk + 1) * ATTN_BLOCK)
            s_ref[u, :, cols] = s[:, cols] if blk == 1 else s[:, cols] + bias_ref[bias_idx, :, cols]

    for u, (j, g, pair, half) in enumerate(units):
        s = s_ref[u]
        sink = sink_ref[4 * g + 2 * pair + half] * LOG2_E
        m = jnp.maximum(jnp.max(s, axis=-1, keepdims=True), sink)
        p = jnp.exp2(s - m)
        denom = jnp.sum(p, axis=-1, keepdims=True) + jnp.exp2(sink - m)
        p_ref[u] = p.astype(BF16)
        r_ref[u] = jnp.broadcast_to(1.0 / denom, (ATTN_BLOCK, LANES))

    pending = []
    for u, (j, g, pair, half) in enumerate(units):
        out = _dot(p_ref[u], vw_ref[win_rows(j), g * LANES:(g + 1) * LANES]) * r_ref[u]
        if half == 0:
            pending.append(out)
        else:
            acc_ref[rows_of(j), pair_cols(g, pair)] = jnp.where(out_lo, pending.pop(), out)

    o_ref[0] = _rmsnorm(acc_ref[...], gain_ref[...]).astype(BF16)


def _attn(q, k, v, sink, gain):
    b, s, _ = q.shape
    ts = min(SEQ_TILE, s)
    nt = s // ts
    bps = ts // ATTN_BLOCK
    nb = s // ATTN_BLOCK
    main = lambda i, t: (i, t, 0)
    prev = lambda i, t: (i, jnp.maximum(t * bps - 1, 0), 0)
    nxt = lambda i, t: (i, jnp.minimum((t + 1) * bps, nb - 1), 0)
    kv_specs = [pl.BlockSpec((1, ATTN_BLOCK, KV_DUP_WIDTH), prev),
                pl.BlockSpec((1, ts, KV_DUP_WIDTH), main),
                pl.BlockSpec((1, ATTN_BLOCK, KV_DUP_WIDTH), nxt)]
    return pl.pallas_call(
        _attn_kernel,
        grid=(b, nt),
        in_specs=[pl.BlockSpec(memory_space=pltpu.SMEM),
                  pl.BlockSpec((1, ts, ATTN_WIDTH), main)] + kv_specs + kv_specs
                 + [pl.BlockSpec((1, ATTN_WIDTH), lambda i, t: (0, 0))],
        out_specs=pl.BlockSpec((1, ts, ATTN_WIDTH), main),
        out_shape=jax.ShapeDtypeStruct((b, s, ATTN_WIDTH), BF16),
        scratch_shapes=[pltpu.VMEM((ts + 2 * ATTN_BLOCK, KV_DUP_WIDTH), BF16),
                        pltpu.VMEM((ts + 2 * ATTN_BLOCK, KV_DUP_WIDTH), BF16),
                        pltpu.VMEM((ts + 2 * ATTN_BLOCK, KV_DUP_WIDTH), BF16),
                        pltpu.VMEM((3, ATTN_BLOCK, 3 * ATTN_BLOCK), F32),
                        pltpu.VMEM((bps * N_ATTN_HEADS, ATTN_BLOCK, 3 * ATTN_BLOCK), F32),
                        pltpu.VMEM((bps * N_ATTN_HEADS, ATTN_BLOCK, 3 * ATTN_BLOCK), BF16),
                        pltpu.VMEM((bps * N_ATTN_HEADS, ATTN_BLOCK, LANES), F32),
                        pltpu.VMEM((ts, ATTN_WIDTH), F32)],
        compiler_params=pltpu.CompilerParams(dimension_semantics=("parallel", "arbitrary"),
                                             vmem_limit_bytes=VMEM_LIMIT),
        name="attn",
    )(sink, q, k, k, k, v, v, v, gain)


def _head_lane_values(ld_ref):
    lane = lax.broadcasted_iota(jnp.int32, (1, RET_WIDTH), 1)
    ld = jnp.zeros((1, RET_WIDTH), F32)
    for h in range(N_RET_HEADS):
        ld = jnp.where(lane >= h * RET_HEAD_DIM, ld_ref[h], ld)
    return ld


def _chunk_decay(ld_ref):
    return jnp.exp(_head_lane_values(ld_ref) * float(RET_CHUNK))


def _head_cols(h):
    return slice(h * RET_HEAD_DIM, (h + 1) * RET_HEAD_DIM)


def _chunk_rows(c):
    return slice(c * RET_CHUNK, (c + 1) * RET_CHUNK)


def _ret_state_kernel(ldb_ref, rkb_ref, rv_ref, sb_ref, state_ref, kv_ref):
    t = pl.program_id(1)
    nchunk = rkb_ref.shape[1] // RET_CHUNK

    @pl.when(t == 0)
    def _():
        state_ref[...] = jnp.zeros_like(state_ref)

    for c in range(nchunk):
        for h in range(N_RET_HEADS):
            kv_ref[c, :, _head_cols(h)] = _dot_tn(rkb_ref[0, _chunk_rows(c), _head_cols(h)],
                                                  rv_ref[0, _chunk_rows(c), _head_cols(h)])

    chunk_decay = _chunk_decay(ldb_ref)
    state = state_ref[...]
    for c in reversed(range(nchunk)):
        sb_ref[0, c] = state.astype(BF16)
        state = state * chunk_decay + kv_ref[c]
    state_ref[...] = state


def _ret_state(rkb, rv, ld_b):
    b, s, _ = rkb.shape
    ts = min(RET_STATE_TILE, s)
    nt = s // ts
    cpt = ts // RET_CHUNK
    rev = lambda i, t: (i, nt - 1 - t, 0)
    return pl.pallas_call(
        _ret_state_kernel,
        grid=(b, nt),
        in_specs=[pl.BlockSpec(memory_space=pltpu.SMEM),
                  pl.BlockSpec((1, ts, RET_WIDTH), rev),
                  pl.BlockSpec((1, ts, RET_WIDTH), rev)],
        out_specs=pl.BlockSpec((1, cpt, RET_HEAD_DIM, RET_WIDTH), lambda i, t: (i, nt - 1 - t, 0, 0)),
        out_shape=jax.ShapeDtypeStruct((b, s // RET_CHUNK, RET_HEAD_DIM, RET_WIDTH), BF16),
        scratch_shapes=[pltpu.VMEM((RET_HEAD_DIM, RET_WIDTH), F32),
                        pltpu.VMEM((cpt, RET_HEAD_DIM, RET_WIDTH), F32)],
        compiler_params=pltpu.CompilerParams(dimension_semantics=("parallel", "arbitrary"),
                                             vmem_limit_bytes=VMEM_LIMIT),
        name="ret_state",
    )(ld_b, rkb, rv)


def _ret_out_kernel(ldf_ref, ldb_ref, rq_ref, rk_ref, rkf_ref, rv_ref, gate_ref, sb_ref, o_ref,
                    state_ref, tab_ref, inner_ref, kv_ref, cs_ref):
    t = pl.program_id(1)
    nchunk = rq_ref.shape[1] // RET_CHUNK
    MASK, Q_FWD, Q_BWD = range(3)

    @pl.when(t == 0)
    def _():
        state_ref[...] = jnp.zeros_like(state_ref)
        shape = (RET_CHUNK, RET_HEAD_DIM)
        row_i = lax.broadcasted_iota(jnp.int32, shape, 0)
        col_i = lax.broadcasted_iota(jnp.int32, shape, 1)
        row = row_i.astype(F32)
        diff = (row_i - col_i).astype(F32)
        for h in range(N_RET_HEADS):
            d_fwd = jnp.where(diff >= 0, jnp.exp(ldf_ref[h] * jnp.maximum(diff, 0.0)), 0.0)
            d_bwd = jnp.where(diff <= 0, jnp.exp(ldb_ref[h] * jnp.maximum(-diff, 0.0)), 0.0)
            tab_ref[MASK, h] = d_fwd + d_bwd
            tab_ref[Q_FWD, h] = jnp.exp(ldf_ref[h] * (row + 1.0))
            tab_ref[Q_BWD, h] = jnp.exp(ldb_ref[h] * (float(RET_CHUNK) - row))

    units = [(c, h) for c in range(nchunk) for h in range(N_RET_HEADS)]

    for u, (c, h) in enumerate(units):
        q = rq_ref[0, _chunk_rows(c), _head_cols(h)]
        k = rk_ref[0, _chunk_rows(c), _head_cols(h)]
        inner_ref[u] = (_dot_nt(q, k) * tab_ref[MASK, h]).astype(BF16)
        kv_ref[c, :, _head_cols(h)] = _dot_tn(rkf_ref[0, _chunk_rows(c), _head_cols(h)],
                                              rv_ref[0, _chunk_rows(c), _head_cols(h)])

    chunk_decay = _chunk_decay(ldf_ref)
    state = state_ref[...]
    for c in range(nchunk):
        fwd = state.astype(BF16)
        for h in range(N_RET_HEADS):
            base = 2 * h * RET_HEAD_DIM
            cs_ref[c, :, base:base + RET_HEAD_DIM] = fwd[:, _head_cols(h)]
            cs_ref[c, :, base + RET_HEAD_DIM:base + 2 * RET_HEAD_DIM] = sb_ref[0, c, :, _head_cols(h)]
        state = state * chunk_decay + kv_ref[c]
    state_ref[...] = state

    for u, (c, h) in enumerate(units):
        q = rq_ref[0, _chunk_rows(c), _head_cols(h)]
        o = _dot(inner_ref[u], rv_ref[0, _chunk_rows(c), _head_cols(h)])
        cross = _dot(q, cs_ref[c, :, 2 * h * RET_HEAD_DIM:2 * (h + 1) * RET_HEAD_DIM])
        o += tab_ref[Q_FWD, h] * cross[:, :RET_HEAD_DIM] + tab_ref[Q_BWD, h] * cross[:, RET_HEAD_DIM:]
        mu = jnp.mean(o, axis=-1, keepdims=True)
        var = jnp.mean(jnp.square(o - mu), axis=-1, keepdims=True)
        on = (o - mu) * lax.rsqrt(var + EPS)
        gate = gate_ref[0, _chunk_rows(c), _head_cols(h)].astype(F32)
        o_ref[0, _chunk_rows(c), _head_cols(h)] = (gate * on).astype(BF16)


def _ret_out(rq, rk, rkf, rv, gate, sb, ld_f, ld_b):
    b, s, _ = rq.shape
    ts = min(SEQ_TILE, s)
    nt = s // ts
    cpt = ts // RET_CHUNK
    main = lambda i, t: (i, t, 0)
    seq_spec = pl.BlockSpec((1, ts, RET_WIDTH), main)
    return pl.pallas_call(
        _ret_out_kernel,
        grid=(b, nt),
        in_specs=[pl.BlockSpec(memory_space=pltpu.SMEM), pl.BlockSpec(memory_space=pltpu.SMEM),
                  seq_spec, seq_spec, seq_spec, seq_spec, seq_spec,
                  pl.BlockSpec((1, cpt, RET_HEAD_DIM, RET_WIDTH), lambda i, t: (i, t, 0, 0))],
        out_specs=seq_spec,
        out_shape=jax.ShapeDtypeStruct((b, s, RET_WIDTH), BF16),
        scratch_shapes=[pltpu.VMEM((RET_HEAD_DIM, RET_WIDTH), F32),
                        pltpu.VMEM((3, N_RET_HEADS, RET_CHUNK, RET_HEAD_DIM), F32),
                        pltpu.VMEM((cpt * N_RET_HEADS, RET_CHUNK, RET_CHUNK), BF16),
                        pltpu.VMEM((cpt, RET_HEAD_DIM, RET_WIDTH), F32),
                        pltpu.VMEM((cpt, RET_HEAD_DIM, 2 * RET_WIDTH), BF16)],
        compiler_params=pltpu.CompilerParams(dimension_semantics=("parallel", "arbitrary"),
                                             vmem_limit_bytes=VMEM_LIMIT),
        name="ret_out",
    )(ld_f, ld_b, rq, rk, rkf, rv, gate, sb)


def _out_ffn_kernel(h_ref, a_ref, r_ref, woa_ref, wor_ref, g2_ref, wg_ref, wu_ref, wd_ref, gf_ref,
                    y_ref, act_ref):
    h = h_ref[...] + _dot(a_ref[...], woa_ref[...]) + _dot(r_ref[...], wor_ref[...])
    hn = _rmsnorm(h, g2_ref[...]).astype(BF16)
    h = h + 0.5 * _swiglu(hn, wg_ref, wu_ref, wd_ref, act_ref)
    y_ref[...] = _rmsnorm(h, gf_ref[...])


def _out_ffn(h, a, r, woa, wor, g2, wg, wu, wd, gf):
    n = h.shape[0]
    tm = min(TOKEN_TILE, n)
    row = lambda i: (i, 0)
    return pl.pallas_call(
        _out_ffn_kernel,
        grid=(n // tm,),
        in_specs=[
            pl.BlockSpec((tm, D_MODEL), row),
            pl.BlockSpec((tm, ATTN_WIDTH), row),
            pl.BlockSpec((tm, RET_WIDTH), row),
            _const_spec((ATTN_WIDTH, D_MODEL)),
            _const_spec((RET_WIDTH, D_MODEL)),
            _const_spec((1, D_MODEL)),
            _const_spec((D_MODEL, D_FF)),
            _const_spec((D_MODEL, D_FF)),
            _const_spec((D_FF, D_MODEL)),
            _const_spec((1, D_MODEL)),
        ],
        out_specs=pl.BlockSpec((tm, D_MODEL), row),
        out_shape=jax.ShapeDtypeStruct((n, D_MODEL), F32),
        scratch_shapes=[pltpu.VMEM((tm, D_FF), BF16)],
        compiler_params=pltpu.CompilerParams(dimension_semantics=("parallel",),
                                             vmem_limit_bytes=VMEM_LIMIT),
        name="out_ffn",
    )(h, a, r, woa, wor, g2, wg, wu, wd, gf)


def _rope_tables(seq_len):
    half = ROT_DIM // 2
    lane = jnp.arange(LANES) % ATTN_HEAD_DIM
    inv_a = ROPE_THETA ** (-jnp.arange(0, ROT_DIM, 2, dtype=F32) / ROT_DIM)
    freq_a = jnp.where(lane < ROT_DIM, inv_a[lane % half], 0.0)
    inv_r = RET_ROT_THETA ** (-jnp.linspace(0.0, 1.0, RET_HEAD_DIM // 2, dtype=F32))
    freq_r = jnp.concatenate([inv_r, inv_r])

    start = (jnp.arange(seq_len // ROPE_BLOCK, dtype=F32) * ROPE_BLOCK)[:, None, None]
    offset = jnp.arange(ROPE_BLOCK, dtype=F32)[None, :, None]

    def cos_sin(freq):
        a, b = start * freq, offset * freq
        cos = jnp.cos(a) * jnp.cos(b) - jnp.sin(a) * jnp.sin(b)
        sin = jnp.sin(a) * jnp.cos(b) + jnp.cos(a) * jnp.sin(b)
        return cos.reshape(seq_len, LANES), sin.reshape(seq_len, LANES)

    cos_a, sin = cos_sin(freq_a)
    sin_a = sin * jnp.where(lane < half, -1.0, 0.0)[None, :]
    sin_b = sin * jnp.where((lane >= half) & (lane < ROT_DIM), 1.0, 0.0)[None, :]
    cos_r, sin = cos_sin(freq_r)
    sign_r = jnp.where(jnp.arange(LANES) < RET_HEAD_DIM // 2, -1.0, 1.0)
    return cos_a, sin_a, sin_b, cos_r, sin * sign_r[None, :]


def _dup_kv_columns(w):
    heads = [w[:, h * ATTN_HEAD_DIM:(h + 1) * ATTN_HEAD_DIM] for h in range(N_KV_HEADS)]
    return jnp.concatenate([heads[0], heads[0], heads[1], heads[1]], axis=1)


def _layer(x, p, tables):
    b, s, _ = x.shape
    n = b * s
    h, q, k, v, rq, rk, rkf, rkb, rv, gate = _ffn_proj(
        x.reshape(n, D_MODEL), s, p["g1"], p["wg1"], p["wu1"], p["wd1"], p["gm"],
        p["wq"], p["wkv"], p["wret"], tables,
        p["ldf"], p["ldb"])
    seq = lambda a: a.reshape(b, s, a.shape[-1])
    attn = _attn(seq(q), seq(k), seq(v), p["sink"], p["ga"])
    sb = _ret_state(seq(rkb), seq(rv), p["ldb"])
    ret = _ret_out(seq(rq), seq(rk), seq(rkf), seq(rv), seq(gate), sb, p["ldf"], p["ldb"])
    y = _out_ffn(h, attn.reshape(n, ATTN_WIDTH), ret.reshape(n, RET_WIDTH),
                 p["woa"], p["wor"], p["g2"], p["wg2"], p["wu2"], p["wd2"], p["gf"])
    return y.reshape(b, s, D_MODEL)


def kernel(x_prompt, x_sample, ffn1_norm, ffn1_w_gate, ffn1_w_up, ffn1_w_down, mix_norm, w_in, attn_sink, attn_out_norm, ret_log_decay_fwd, ret_log_decay_bwd, w_out, ffn2_norm, ffn2_w_gate, ffn2_w_up, ffn2_w_down, final_norm):
    assert ffn1_norm.shape[0] == 1
    o1 = ATTN_WIDTH
    o2 = o1 + N_KV_HEADS * ATTN_HEAD_DIM
    o3 = o2 + N_KV_HEADS * ATTN_HEAD_DIM
    wi = w_in[0]
    wkv = jnp.concatenate([_dup_kv_columns(wi[:, o1:o2].astype(BF16)),
                           _dup_kv_columns(wi[:, o2:o3].astype(BF16))], axis=1)
    wo = w_out[0].astype(BF16)
    p = dict(
        g1=ffn1_norm[0][None, :], wg1=ffn1_w_gate[0].astype(BF16), wu1=ffn1_w_up[0].astype(BF16),
        wd1=ffn1_w_down[0].astype(BF16), gm=mix_norm[0][None, :],
        wq=wi[:, :o1].astype(BF16), wkv=wkv, wret=wi[:, o3:].astype(BF16),
        sink=attn_sink[0], ga=attn_out_norm[0][None, :],
        ldf=ret_log_decay_fwd[0], ldb=ret_log_decay_bwd[0],
        woa=wo[:ATTN_WIDTH], wor=wo[ATTN_WIDTH:],
        g2=ffn2_norm[0][None, :], wg2=ffn2_w_gate[0].astype(BF16), wu2=ffn2_w_up[0].astype(BF16),
        wd2=ffn2_w_down[0].astype(BF16), gf=final_norm[None, :],
    )
    tables = _rope_tables(max(x_prompt.shape[1], x_sample.shape[1]))
    return (_layer(x_prompt, p, tables), _layer(x_sample, p, tables))
```

```python
import functools

import jax
import jax.numpy as jnp
from jax import lax
from jax.experimental import pallas as pl
from jax.experimental.pallas import tpu as pltpu

D_MODEL = 1024
D_FF = 2816
N_ATTN_HEADS = 8
N_KV_HEADS = 2
ATTN_HEAD_DIM = 64
ATTN_WIDTH = N_ATTN_HEADS * ATTN_HEAD_DIM
WINDOW = 128
ATTN_BLOCK = 128
ROT_DIM = 16
ROPE_THETA = 500000.0
RET_HEAD_DIM = 128
N_RET_HEADS = 4
RET_WIDTH = N_RET_HEADS * RET_HEAD_DIM
RET_CHUNK = 128
RET_ROT_THETA = 10000.0
EPS = 1e-6
NEG_BIG = -1e30
LOG2_E = 1.4426950408889634
ATTN_Q_SCALE = ATTN_HEAD_DIM ** -0.5 * LOG2_E

LANES = 128
KV_DUP_WIDTH = 2 * LANES
FF_CHUNK = 256
PROJ_CHUNK = 512
ROW_GROUPS = 2
TOKEN_TILE = 512
SEQ_TILE = 1024
RET_STATE_TILE = 2048
ROPE_BLOCK = 128
VMEM_LIMIT = 56 * 1024 * 1024

BF16 = jnp.bfloat16
F32 = jnp.float32


def _dot(a, b):
    return jnp.dot(a, b, preferred_element_type=F32)


def _dot_nt(a, b):
    return lax.dot_general(a, b, (((1,), (1,)), ((), ())), preferred_element_type=F32)


def _dot_tn(a, b):
    return lax.dot_general(a, b, (((0,), (0,)), ((), ())), preferred_element_type=F32)


def _rmsnorm(x, g):
    return x * lax.rsqrt(jnp.mean(x * x, axis=-1, keepdims=True) + EPS) * g


def _silu(x):
    return x * (1.0 / (1.0 + jnp.exp(-x)))


def _row_groups(rows):
    size = rows // ROW_GROUPS
    return [slice(i * size, (i + 1) * size) for i in range(ROW_GROUPS)]


def _gate_up(xn, rows, wg_ref, wu_ref, act_ref):
    for c in range(D_FF // FF_CHUNK):
        cols = slice(c * FF_CHUNK, (c + 1) * FF_CHUNK)
        g = _dot(xn, wg_ref[:, cols])
        u = _dot(xn, wu_ref[:, cols])
        act_ref[rows, cols] = (_silu(g) * u).astype(BF16)


def _ffn_proj_kernel(ldf_ref, ldb_ref, x_ref, g1_ref, wg_ref, wu_ref, wd_ref, gm_ref,
                     wq_ref, wkv_ref, wret_ref,
                     cosa_ref, sina_ref, sinb_ref, cosr_ref, sinr_ref,
                     h_ref, q_ref, k_ref, v_ref, rq_ref, rk_ref, rkf_ref, rkb_ref, rv_ref, gate_ref,
                     act_ref, decay_ref):
    @pl.when(pl.program_id(0) == 0)
    def _():
        shape = decay_ref.shape[1:]
        j = (lax.broadcasted_iota(jnp.int32, shape, 0) % RET_CHUNK).astype(F32)
        decay_ref[0] = jnp.exp(_head_lane_values(ldf_ref) * (float(RET_CHUNK) - 1.0 - j))
        decay_ref[1] = jnp.exp(_head_lane_values(ldb_ref) * j)

    groups = _row_groups(x_ref.shape[0])
    xns = [_rmsnorm(x_ref[r, :], g1_ref[...]).astype(BF16) for r in groups]
    for r, xn in zip(groups, xns):
        _gate_up(xn, r, wg_ref, wu_ref, act_ref)
    ys = [_dot(act_ref[r, :], wd_ref[...]) for r in groups]

    def lane_blocks(p):
        return [p[:, c * LANES:(c + 1) * LANES] for c in range(p.shape[1] // LANES)]

    for r, y in zip(groups, ys):
        h = x_ref[r, :] + 0.5 * y
        h_ref[r, :] = h
        un = _rmsnorm(h, gm_ref[...]).astype(BF16)

        cosa, sina, sinb = cosa_ref[r, :], sina_ref[r, :], sinb_ref[r, :]
        cosr, sinr = cosr_ref[r, :], sinr_ref[r, :]

        def rope_a(xb):
            return (xb * cosa + pltpu.roll(xb, LANES - ROT_DIM // 2, 1) * sina
                    + pltpu.roll(xb, ROT_DIM // 2, 1) * sinb)

        def rope_r(xb):
            return xb * cosr + pltpu.roll(xb, RET_HEAD_DIM // 2, 1) * sinr

        def proj(section):
            if section == 0:
                return _dot(un, wq_ref[...])
            if section == 1:
                return _dot(un, wkv_ref[...])
            return _dot(un, wret_ref[:, (section - 2) * PROJ_CHUNK:(section - 1) * PROJ_CHUNK])

        for c, blk in enumerate(lane_blocks(proj(3))):
            cols = slice(c * LANES, (c + 1) * LANES)
            rk = rope_r(blk) * (RET_HEAD_DIM ** -0.5)
            rk_ref[r, cols] = rk.astype(BF16)
            rkf_ref[r, cols] = (rk * decay_ref[0, r, cols]).astype(BF16)
            rkb_ref[r, cols] = (rk * decay_ref[1, r, cols]).astype(BF16)
        gate_ref[r, :] = _silu(proj(5)).astype(BF16)
        for c, blk in enumerate(lane_blocks(proj(0))):
            q_ref[r, c * LANES:(c + 1) * LANES] = (rope_a(blk) * ATTN_Q_SCALE).astype(BF16)
        for c, blk in enumerate(lane_blocks(proj(2))):
            rq_ref[r, c * LANES:(c + 1) * LANES] = rope_r(blk).astype(BF16)
        kv = lane_blocks(proj(1))
        for c, blk in enumerate(kv[:KV_DUP_WIDTH // LANES]):
            k_ref[r, c * LANES:(c + 1) * LANES] = rope_a(blk).astype(BF16)
        for c, blk in enumerate(kv[KV_DUP_WIDTH // LANES:]):
            v_ref[r, c * LANES:(c + 1) * LANES] = blk.astype(BF16)
        rv_ref[r, :] = proj(4).astype(BF16)


def _const_spec(shape):
    return pl.BlockSpec(shape, lambda *_: (0,) * len(shape), pipeline_mode=pl.Buffered(1))


def _ffn_proj(x2d, seq_len, g1, wg, wu, wd, gm, wq, wkv, wret, tables, ld_f, ld_b):
    n = x2d.shape[0]
    tm = min(TOKEN_TILE, seq_len)
    assert tm % RET_CHUNK == 0
    tiles_per_seq = seq_len // tm
    row = lambda i: (i, 0)
    tab = lambda i: (i % tiles_per_seq, 0)
    out_widths = (ATTN_WIDTH, KV_DUP_WIDTH, KV_DUP_WIDTH) + (RET_WIDTH,) * 6
    return pl.pallas_call(
        _ffn_proj_kernel,
        grid=(n // tm,),
        in_specs=[
            pl.BlockSpec(memory_space=pltpu.SMEM),
            pl.BlockSpec(memory_space=pltpu.SMEM),
            pl.BlockSpec((tm, D_MODEL), row),
            _const_spec((1, D_MODEL)),
            _const_spec((D_MODEL, D_FF)),
            _const_spec((D_MODEL, D_FF)),
            _const_spec((D_FF, D_MODEL)),
            _const_spec((1, D_MODEL)),
            _const_spec(wq.shape),
            _const_spec(wkv.shape),
            _const_spec(wret.shape),
        ] + [pl.BlockSpec((tm, LANES), tab)] * 5,
        out_specs=[pl.BlockSpec((tm, D_MODEL), row)] + [pl.BlockSpec((tm, w), row) for w in out_widths],
        out_shape=[jax.ShapeDtypeStruct((n, D_MODEL), F32)]
                  + [jax.ShapeDtypeStruct((n, w), BF16) for w in out_widths],
        scratch_shapes=[pltpu.VMEM((tm, D_FF), BF16),
                        pltpu.VMEM((2, tm, RET_WIDTH), F32)],
        compiler_params=pltpu.CompilerParams(dimension_semantics=("arbitrary",),
                                             vmem_limit_bytes=VMEM_LIMIT),
        name="ffn_proj",
    )(ld_f, ld_b, x2d, g1, wg, wu, wd, gm, wq, wkv, wret, *tables)


def _attn_kernel(sink_ref, q_ref, kp_ref, km_ref, kn_ref, vp_ref, vm_ref, vn_ref, gain_ref,
                 o_ref, klo_ref, khi_ref, vw_ref, bias_ref, s_ref, p_ref, r_ref, acc_ref):
    t = pl.program_id(1)
    nt = pl.num_programs(1)
    ts = q_ref.shape[1]
    nblk = ts // ATTN_BLOCK
    wk = 3 * ATTN_BLOCK

    @pl.when(t == 0)
    def _():
        qi = lax.broadcasted_iota(jnp.int32, (ATTN_BLOCK, wk), 0)
        kj = lax.broadcasted_iota(jnp.int32, (ATTN_BLOCK, wk), 1)
        band = (kj >= qi) & (kj <= qi + 2 * WINDOW)
        bias_ref[0] = jnp.where(band, 0.0, NEG_BIG)
        bias_ref[1] = jnp.where(band & (kj >= ATTN_BLOCK), 0.0, NEG_BIG)
        bias_ref[2] = jnp.where(band & (kj < 2 * ATTN_BLOCK), 0.0, NEG_BIG)

    lo_lanes = (lax.broadcasted_iota(jnp.int32, (1, KV_DUP_WIDTH), 1) % LANES) < ATTN_HEAD_DIM
    for dst, src in ((slice(0, ATTN_BLOCK), kp_ref), (slice(ATTN_BLOCK, ATTN_BLOCK + ts), km_ref),
                     (slice(ATTN_BLOCK + ts, 2 * ATTN_BLOCK + ts), kn_ref)):
        kblk = src[0]
        klo_ref[dst, :] = jnp.where(lo_lanes, kblk, jnp.zeros_like(kblk))
        khi_ref[dst, :] = jnp.where(lo_lanes, jnp.zeros_like(kblk), kblk)
    vw_ref[0:ATTN_BLOCK, :] = vp_ref[0]
    vw_ref[ATTN_BLOCK:ATTN_BLOCK + ts, :] = vm_ref[0]
    vw_ref[ATTN_BLOCK + ts:, :] = vn_ref[0]

    out_lo = lax.broadcasted_iota(jnp.int32, (ATTN_BLOCK, LANES), 1) < ATTN_HEAD_DIM
    first_idx = jnp.where(t == 0, 1, 0)
    last_idx = jnp.where(t == nt - 1, 2, 0)

    units = [(j, g, pair, half) for j in range(nblk) for g in range(N_KV_HEADS)
             for pair in range(2) for half in range(2)]

    def rows_of(j):
        return slice(j * ATTN_BLOCK, (j + 1) * ATTN_BLOCK)

    def win_rows(j):
        return slice(j * ATTN_BLOCK, j * ATTN_BLOCK + wk)

    def pair_cols(g, pair):
        return slice((2 * g + pair) * LANES, (2 * g + pair + 1) * LANES)

    for u, (j, g, pair, half) in enumerate(units):
        k_ref = khi_ref if half else klo_ref
        bias_idx = first_idx if j == 0 else (last_idx if j == nblk - 1 else 0)
        s = _dot_nt(q_ref[0, rows_of(j), pair_cols(g, pair)],
                    k_ref[win_rows(j), g * LANES:(g + 1) * LANES])
        for blk in range(3):
            cols = slice(blk * ATTN_BLOCK, (blk + 1) * ATTN_BLOCK)
            s_ref[u, :, cols] = s[:, cols] if blk == 1 else s[:, cols] + bias_ref[bias_idx, :, cols]

    for u, (j, g, pair, half) in enumerate(units):
        s = s_ref[u]
        sink = sink_ref[4 * g + 2 * pair + half] * LOG2_E
        m = jnp.maximum(jnp.max(s, axis=-1, keepdims=True), sink)
        p = jnp.exp2(s - m)
        denom = jnp.sum(p, axis=-1, keepdims=True) + jnp.exp2(sink - m)
        p_ref[u] = p.astype(BF16)
        r_ref[u] = jnp.broadcast_to(1.0 / denom, (ATTN_BLOCK, LANES))

    pending = []
    for u, (j, g, pair, half) in enumerate(units):
        out = _dot(p_ref[u], vw_ref[win_rows(j), g * LANES:(g + 1) * LANES]) * r_ref[u]
        if half == 0:
            pending.append(out)
        else:
            acc_ref[rows_of(j), pair_cols(g, pair)] = jnp.where(out_lo, pending.pop(), out)

    o_ref[0] = _rmsnorm(acc_ref[...], gain_ref[...]).astype(BF16)


def _attn(q, k, v, sink, gain):
    b, s, _ = q.shape
    ts = min(SEQ_TILE, s)
    nt = s // ts
    bps = ts // ATTN_BLOCK
    nb = s // ATTN_BLOCK
    main = lambda i, t: (i, t, 0)
    prev = lambda i, t: (i, jnp.maximum(t * bps - 1, 0), 0)
    nxt = lambda i, t: (i, jnp.minimum((t + 1) * bps, nb - 1), 0)
    kv_specs = [pl.BlockSpec((1, ATTN_BLOCK, KV_DUP_WIDTH), prev),
                pl.BlockSpec((1, ts, KV_DUP_WIDTH), main),
                pl.BlockSpec((1, ATTN_BLOCK, KV_DUP_WIDTH), nxt)]
    return pl.pallas_call(
        _attn_kernel,
        grid=(b, nt),
        in_specs=[pl.BlockSpec(memory_space=pltpu.SMEM),
                  pl.BlockSpec((1, ts, ATTN_WIDTH), main)] + kv_specs + kv_specs
                 + [pl.BlockSpec((1, ATTN_WIDTH), lambda i, t: (0, 0))],
        out_specs=pl.BlockSpec((1, ts, ATTN_WIDTH), main),
        out_shape=jax.ShapeDtypeStruct((b, s, ATTN_WIDTH), BF16),
        scratch_shapes=[pltpu.VMEM((ts + 2 * ATTN_BLOCK, KV_DUP_WIDTH), BF16),
                        pltpu.VMEM((ts + 2 * ATTN_BLOCK, KV_DUP_WIDTH), BF16),
                        pltpu.VMEM((ts + 2 * ATTN_BLOCK, KV_DUP_WIDTH), BF16),
                        pltpu.VMEM((3, ATTN_BLOCK, 3 * ATTN_BLOCK), F32),
                        pltpu.VMEM((bps * N_ATTN_HEADS, ATTN_BLOCK, 3 * ATTN_BLOCK), F32),
                        pltpu.VMEM((bps * N_ATTN_HEADS, ATTN_BLOCK, 3 * ATTN_BLOCK), BF16),
                        pltpu.VMEM((bps * N_ATTN_HEADS, ATTN_BLOCK, LANES), F32),
                        pltpu.VMEM((ts, ATTN_WIDTH), F32)],
        compiler_params=pltpu.CompilerParams(dimension_semantics=("parallel", "arbitrary"),
                                             vmem_limit_bytes=VMEM_LIMIT),
        name="attn",
    )(sink, q, k, k, k, v, v, v, gain)


def _head_lane_values(ld_ref):
    lane = lax.broadcasted_iota(jnp.int32, (1, RET_WIDTH), 1)
    ld = jnp.zeros((1, RET_WIDTH), F32)
    for h in range(N_RET_HEADS):
        ld = jnp.where(lane >= h * RET_HEAD_DIM, ld_ref[h], ld)
    return ld


def _chunk_decay(ld_ref):
    return jnp.exp(_head_lane_values(ld_ref) * float(RET_CHUNK))


def _head_cols(h):
    return slice(h * RET_HEAD_DIM, (h + 1) * RET_HEAD_DIM)


def _chunk_rows(c):
    return slice(c * RET_CHUNK, (c + 1) * RET_CHUNK)


def _ret_state_kernel(ldb_ref, rkb_ref, rv_ref, sb_ref, state_ref, kv_ref):
    t = pl.program_id(1)
    nchunk = rkb_ref.shape[1] // RET_CHUNK

    @pl.when(t == 0)
    def _():
        state_ref[...] = jnp.zeros_like(state_ref)

    for c in range(nchunk):
        for h in range(N_RET_HEADS):
            kv_ref[c, :, _head_cols(h)] = _dot_tn(rkb_ref[0, _chunk_rows(c), _head_cols(h)],
                                                  rv_ref[0, _chunk_rows(c), _head_cols(h)])

    chunk_decay = _chunk_decay(ldb_ref)
    state = state_ref[...]
    for c in reversed(range(nchunk)):
        sb_ref[0, c] = state.astype(BF16)
        state = state * chunk_decay + kv_ref[c]
    state_ref[...] = state


def _ret_state(rkb, rv, ld_b):
    b, s, _ = rkb.shape
    ts = min(RET_STATE_TILE, s)
    nt = s // ts
    cpt = ts // RET_CHUNK
    rev = lambda i, t: (i, nt - 1 - t, 0)
    return pl.pallas_call(
        _ret_state_kernel,
        grid=(b, nt),
        in_specs=[pl.BlockSpec(memory_space=pltpu.SMEM),
                  pl.BlockSpec((1, ts, RET_WIDTH), rev),
                  pl.BlockSpec((1, ts, RET_WIDTH), rev)],
        out_specs=pl.BlockSpec((1, cpt, RET_HEAD_DIM, RET_WIDTH), lambda i, t: (i, nt - 1 - t, 0, 0)),
        out_shape=jax.ShapeDtypeStruct((b, s // RET_CHUNK, RET_HEAD_DIM, RET_WIDTH), BF16),
        scratch_shapes=[pltpu.VMEM((RET_HEAD_DIM, RET_WIDTH), F32),
                        pltpu.VMEM((cpt, RET_HEAD_DIM, RET_WIDTH), F32)],
        compiler_params=pltpu.CompilerParams(dimension_semantics=("parallel", "arbitrary"),
                                             vmem_limit_bytes=VMEM_LIMIT),
        name="ret_state",
    )(ld_b, rkb, rv)


def _ret_out_kernel(ldf_ref, ldb_ref, rq_ref, rk_ref, rkf_ref, rv_ref, gate_ref, sb_ref, o_ref,
                    state_ref, tab_ref, inner_ref, kv_ref, cs_ref):
    t = pl.program_id(1)
    nchunk = rq_ref.shape[1] // RET_CHUNK
    MASK, Q_FWD, Q_BWD = range(3)

    @pl.when(t == 0)
    def _():
        state_ref[...] = jnp.zeros_like(state_ref)
        shape = (RET_CHUNK, RET_HEAD_DIM)
        row_i = lax.broadcasted_iota(jnp.int32, shape, 0)
        col_i = lax.broadcasted_iota(jnp.int32, shape, 1)
        row = row_i.astype(F32)
        diff = (row_i - col_i).astype(F32)
        for h in range(N_RET_HEADS):
            d_fwd = jnp.where(diff >= 0, jnp.exp(ldf_ref[h] * jnp.maximum(diff, 0.0)), 0.0)
            d_bwd = jnp.where(diff <= 0, jnp.exp(ldb_ref[h] * jnp.maximum(-diff, 0.0)), 0.0)
            tab_ref[MASK, h] = d_fwd + d_bwd
            tab_ref[Q_FWD, h] = jnp.exp(ldf_ref[h] * (row + 1.0))
            tab_ref[Q_BWD, h] = jnp.exp(ldb_ref[h] * (float(RET_CHUNK) - row))

    units = [(c, h) for c in range(nchunk) for h in range(N_RET_HEADS)]

    for u, (c, h) in enumerate(units):
        q = rq_ref[0, _chunk_rows(c), _head_cols(h)]
        k = rk_ref[0, _chunk_rows(c), _head_cols(h)]
        inner_ref[u] = (_dot_nt(q, k) * tab_ref[MASK, h]).astype(BF16)
        kv_ref[c, :, _head_cols(h)] = _dot_tn(rkf_ref[0, _chunk_rows(c), _head_cols(h)],
                                              rv_ref[0, _chunk_rows(c), _head_cols(h)])

    chunk_decay = _chunk_decay(ldf_ref)
    state = state_ref[...]
    for c in range(nchunk):
        fwd = state.astype(BF16)
        for h in range(N_RET_HEADS):
            base = 2 * h * RET_HEAD_DIM
            cs_ref[c, :, base:base + RET_HEAD_DIM] = fwd[:, _head_cols(h)]
            cs_ref[c, :, base + RET_HEAD_DIM:base + 2 * RET_HEAD_DIM] = sb_ref[0, c, :, _head_cols(h)]
        state = state * chunk_decay + kv_ref[c]
    state_ref[...] = state

    for u, (c, h) in enumerate(units):
        q = rq_ref[0, _chunk_rows(c), _head_cols(h)]
        o = _dot(inner_ref[u], rv_ref[0, _chunk_rows(c), _head_cols(h)])
        cross = _dot(q, cs_ref[c, :, 2 * h * RET_HEAD_DIM:2 * (h + 1) * RET_HEAD_DIM])
        o += tab_ref[Q_FWD, h] * cross[:, :RET_HEAD_DIM] + tab_ref[Q_BWD, h] * cross[:, RET_HEAD_DIM:]
        mu = jnp.mean(o, axis=-1, keepdims=True)
        var = jnp.mean(jnp.square(o - mu), axis=-1, keepdims=True)
        on = (o - mu) * lax.rsqrt(var + EPS)
        gate = gate_ref[0, _chunk_rows(c), _head_cols(h)].astype(F32)
        o_ref[0, _chunk_rows(c), _head_cols(h)] = (gate * on).astype(BF16)


def _ret_out(rq, rk, rkf, rv, gate, sb, ld_f, ld_b):
    b, s, _ = rq.shape
    ts = min(SEQ_TILE, s)
    nt = s // ts
    cpt = ts // RET_CHUNK
    main = lambda i, t: (i, t, 0)
    seq_spec = pl.BlockSpec((1, ts, RET_WIDTH), main)
    return pl.pallas_call(
        _ret_out_kernel,
        grid=(b, nt),
        in_specs=[pl.BlockSpec(memory_space=pltpu.SMEM), pl.BlockSpec(memory_space=pltpu.SMEM),
                  seq_spec, seq_spec, seq_spec, seq_spec, seq_spec,
                  pl.BlockSpec((1, cpt, RET_HEAD_DIM, RET_WIDTH), lambda i, t: (i, t, 0, 0))],
        out_specs=seq_spec,
        out_shape=jax.ShapeDtypeStruct((b, s, RET_WIDTH), BF16),
        scratch_shapes=[pltpu.VMEM((RET_HEAD_DIM, RET_WIDTH), F32),
                        pltpu.VMEM((3, N_RET_HEADS, RET_CHUNK, RET_HEAD_DIM), F32),
                        pltpu.VMEM((cpt * N_RET_HEADS, RET_CHUNK, RET_CHUNK), BF16),
                        pltpu.VMEM((cpt, RET_HEAD_DIM, RET_WIDTH), F32),
                        pltpu.VMEM((cpt, RET_HEAD_DIM, 2 * RET_WIDTH), BF16)],
        compiler_params=pltpu.CompilerParams(dimension_semantics=("parallel", "arbitrary"),
                                             vmem_limit_bytes=VMEM_LIMIT),
        name="ret_out",
    )(ld_f, ld_b, rq, rk, rkf, rv, gate, sb)


def _out_ffn_kernel(h_ref, a_ref, r_ref, woa_ref, wor_ref, g2_ref, wg_ref, wu_ref, wd_ref, gf_ref,
                    y_ref, act_ref):
    groups = _row_groups(h_ref.shape[0])
    hs = [h_ref[r, :] + _dot(a_ref[r, :], woa_ref[...]) + _dot(r_ref[r, :], wor_ref[...]) for r in groups]
    hns = [_rmsnorm(h, g2_ref[...]).astype(BF16) for h in hs]
    for r, hn in zip(groups, hns):
        _gate_up(hn, r, wg_ref, wu_ref, act_ref)
    ys = [_dot(act_ref[r, :], wd_ref[...]) for r in groups]
    for r, h, y in zip(groups, hs, ys):
        y_ref[r, :] = _rmsnorm(h + 0.5 * y, gf_ref[...])


def _out_ffn(h, a, r, woa, wor, g2, wg, wu, wd, gf):
    n = h.shape[0]
    tm = min(TOKEN_TILE, n)
    row = lambda i: (i, 0)
    return pl.pallas_call(
        _out_ffn_kernel,
        grid=(n // tm,),
        in_specs=[
            pl.BlockSpec((tm, D_MODEL), row),
            pl.BlockSpec((tm, ATTN_WIDTH), row),
            pl.BlockSpec((tm, RET_WIDTH), row),
            _const_spec((ATTN_WIDTH, D_MODEL)),
            _const_spec((RET_WIDTH, D_MODEL)),
            _const_spec((1, D_MODEL)),
            _const_spec((D_MODEL, D_FF)),
            _const_spec((D_MODEL, D_FF)),
            _const_spec((D_FF, D_MODEL)),
            _const_spec((1, D_MODEL)),
        ],
        out_specs=pl.BlockSpec((tm, D_MODEL), row),
        out_shape=jax.ShapeDtypeStruct((n, D_MODEL), F32),
        scratch_shapes=[pltpu.VMEM((tm, D_FF), BF16)],
        compiler_params=pltpu.CompilerParams(dimension_semantics=("parallel",),
                                             vmem_limit_bytes=VMEM_LIMIT),
        name="out_ffn",
    )(h, a, r, woa, wor, g2, wg, wu, wd, gf)


def _rope_tables(seq_len):
    half = ROT_DIM // 2
    lane = jnp.arange(LANES) % ATTN_HEAD_DIM
    inv_a = ROPE_THETA ** (-jnp.arange(0, ROT_DIM, 2, dtype=F32) / ROT_DIM)
    freq_a = jnp.where(lane < ROT_DIM, inv_a[lane % half], 0.0)
    inv_r = RET_ROT_THETA ** (-jnp.linspace(0.0, 1.0, RET_HEAD_DIM // 2, dtype=F32))
    freq_r = jnp.concatenate([inv_r, inv_r])

    start = (jnp.arange(seq_len // ROPE_BLOCK, dtype=F32) * ROPE_BLOCK)[:, None, None]
    offset = jnp.arange(ROPE_BLOCK, dtype=F32)[None, :, None]

    def cos_sin(freq):
        a, b = start * freq, offset * freq
        cos = jnp.cos(a) * jnp.cos(b) - jnp.sin(a) * jnp.sin(b)
        sin = jnp.sin(a) * jnp.cos(b) + jnp.cos(a) * jnp.sin(b)
        return cos.reshape(seq_len, LANES), sin.reshape(seq_len, LANES)

    cos_a, sin = cos_sin(freq_a)
    sin_a = sin * jnp.where(lane < half, -1.0, 0.0)[None, :]
    sin_b = sin * jnp.where((lane >= half) & (lane < ROT_DIM), 1.0, 0.0)[None, :]
    cos_r, sin = cos_sin(freq_r)
    sign_r = jnp.where(jnp.arange(LANES) < RET_HEAD_DIM // 2, -1.0, 1.0)
    return cos_a, sin_a, sin_b, cos_r, sin * sign_r[None, :]


def _dup_kv_columns(w):
    heads = [w[:, h * ATTN_HEAD_DIM:(h + 1) * ATTN_HEAD_DIM] for h in range(N_KV_HEADS)]
    return jnp.concatenate([heads[0], heads[0], heads[1], heads[1]], axis=1)


def _layer(x, p, tables):
    b, s, _ = x.shape
    n = b * s
    h, q, k, v, rq, rk, rkf, rkb, rv, gate = _ffn_proj(
        x.reshape(n, D_MODEL), s, p["g1"], p["wg1"], p["wu1"], p["wd1"], p["gm"],
        p["wq"], p["wkv"], p["wret"], tables,
        p["ldf"], p["ldb"])
    seq = lambda a: a.reshape(b, s, a.shape[-1])
    attn = _attn(seq(q), seq(k), seq(v), p["sink"], p["ga"])
    sb = _ret_state(seq(rkb), seq(rv), p["ldb"])
    ret = _ret_out(seq(rq), seq(rk), seq(rkf), seq(rv), seq(gate), sb, p["ldf"], p["ldb"])
    y = _out_ffn(h, attn.reshape(n, ATTN_WIDTH), ret.reshape(n, RET_WIDTH),
                 p["woa"], p["wor"], p["g2"], p["wg2"], p["wu2"], p["wd2"], p["gf"])
    return y.reshape(b, s, D_MODEL)


def kernel(x_prompt, x_sample, ffn1_norm, ffn1_w_gate, ffn1_w_up, ffn1_w_down, mix_norm, w_in, attn_sink, attn_out_norm, ret_log_decay_fwd, ret_log_decay_bwd, w_out, ffn2_norm, ffn2_w_gate, ffn2_w_up, ffn2_w_down, final_norm):
    assert ffn1_norm.shape[0] == 1
    o1 = ATTN_WIDTH
    o2 = o1 + N_KV_HEADS * ATTN_HEAD_DIM
    o3 = o2 + N_KV_HEADS * ATTN_HEAD_DIM
    wi = w_in[0]
    wkv = jnp.concatenate([_dup_kv_columns(wi[:, o1:o2].astype(BF16)),
                           _dup_kv_columns(wi[:, o2:o3].astype(BF16))], axis=1)
    wo = w_out[0].astype(BF16)
    p = dict(
        g1=ffn1_norm[0][None, :], wg1=ffn1_w_gate[0].astype(BF16), wu1=ffn1_w_up[0].astype(BF16),
        wd1=ffn1_w_down[0].astype(BF16), gm=mix_norm[0][None, :],
        wq=wi[:, :o1].astype(BF16), wkv=wkv, wret=wi[:, o3:].astype(BF16),
        sink=attn_sink[0], ga=attn_out_norm[0][None, :],
        ldf=ret_log_decay_fwd[0], ldb=ret_log_decay_bwd[0],
        woa=wo[:ATTN_WIDTH], wor=wo[ATTN_WIDTH:],
        g2=ffn2_norm[0][None, :], wg2=ffn2_w_gate[0].astype(BF16), wu2=ffn2_w_up[0].astype(BF16),
        wd2=ffn2_w_down[0].astype(BF16), gf=final_norm[None, :],
    )
    tables = _rope_tables(max(x_prompt.shape[1], x_sample.shape[1]))
    return (_layer(x_prompt, p, tables), _layer(x_sample, p, tables))
```

```python
import functools

import jax
import jax.numpy as jnp
from jax import lax
from jax.experimental import pallas as pl
from jax.experimental.pallas import tpu as pltpu

D_MODEL = 1024
D_FF = 2816
N_ATTN_HEADS = 8
N_KV_HEADS = 2
ATTN_HEAD_DIM = 64
ATTN_WIDTH = N_ATTN_HEADS * ATTN_HEAD_DIM
WINDOW = 128
ATTN_BLOCK = 128
ROT_DIM = 16
ROPE_THETA = 500000.0
RET_HEAD_DIM = 128
N_RET_HEADS = 4
RET_WIDTH = N_RET_HEADS * RET_HEAD_DIM
RET_CHUNK = 128
RET_ROT_THETA = 10000.0
EPS = 1e-6
NEG_BIG = -1e30
LOG2_E = 1.4426950408889634
ATTN_Q_SCALE = ATTN_HEAD_DIM ** -0.5 * LOG2_E

LANES = 128
KV_DUP_WIDTH = 2 * LANES
FF_CHUNK = 256
PROJ_CHUNK = 512
ROW_GROUP = 256
TOKEN_TILE = 512
OUT_TOKEN_TILE = 1024
SEQ_TILE = 1024
RET_STATE_TILE = 2048
ROPE_BLOCK = 128
VMEM_LIMIT = 56 * 1024 * 1024

BF16 = jnp.bfloat16
F32 = jnp.float32


def _dot(a, b):
    return jnp.dot(a, b, preferred_element_type=F32)


def _dot_nt(a, b):
    return lax.dot_general(a, b, (((1,), (1,)), ((), ())), preferred_element_type=F32)


def _dot_tn(a, b):
    return lax.dot_general(a, b, (((0,), (0,)), ((), ())), preferred_element_type=F32)


def _rmsnorm(x, g):
    return x * lax.rsqrt(jnp.mean(x * x, axis=-1, keepdims=True) + EPS) * g


def _silu(x):
    return x * (1.0 / (1.0 + jnp.exp(-x)))


def _row_groups(rows):
    return [slice(i, i + ROW_GROUP) for i in range(0, rows, ROW_GROUP)]


def _gate_up(xn, rows, wg_ref, wu_ref, act_ref):
    for c in range(D_FF // FF_CHUNK):
        cols = slice(c * FF_CHUNK, (c + 1) * FF_CHUNK)
        g = _dot(xn, wg_ref[:, cols])
        u = _dot(xn, wu_ref[:, cols])
        act_ref[rows, cols] = (_silu(g) * u).astype(BF16)


def _ffn_proj_kernel(ldf_ref, ldb_ref, x_ref, g1_ref, wg_ref, wu_ref, wd_ref, gm_ref,
                     wq_ref, wkv_ref, wret_ref,
                     cosa_ref, sina_ref, sinb_ref, cosr_ref, sinr_ref,
                     h_ref, q_ref, k_ref, v_ref, rq_ref, rk_ref, rkf_ref, rkb_ref, rv_ref, gate_ref,
                     act_ref, decay_ref):
    @pl.when(pl.program_id(0) == 0)
    def _():
        shape = decay_ref.shape[1:]
        j = (lax.broadcasted_iota(jnp.int32, shape, 0) % RET_CHUNK).astype(F32)
        decay_ref[0] = jnp.exp(_head_lane_values(ldf_ref) * (float(RET_CHUNK) - 1.0 - j))
        decay_ref[1] = jnp.exp(_head_lane_values(ldb_ref) * j)

    groups = _row_groups(x_ref.shape[0])
    xns = [_rmsnorm(x_ref[r, :], g1_ref[...]).astype(BF16) for r in groups]
    for r, xn in zip(groups, xns):
        _gate_up(xn, r, wg_ref, wu_ref, act_ref)
    ys = [_dot(act_ref[r, :], wd_ref[...]) for r in groups]

    def lane_blocks(p):
        return [p[:, c * LANES:(c + 1) * LANES] for c in range(p.shape[1] // LANES)]

    for r, y in zip(groups, ys):
        h = x_ref[r, :] + 0.5 * y
        h_ref[r, :] = h
        un = _rmsnorm(h, gm_ref[...]).astype(BF16)

        cosa, sina, sinb = cosa_ref[r, :], sina_ref[r, :], sinb_ref[r, :]
        cosr, sinr = cosr_ref[r, :], sinr_ref[r, :]

        def rope_a(xb):
            return (xb * cosa + pltpu.roll(xb, LANES - ROT_DIM // 2, 1) * sina
                    + pltpu.roll(xb, ROT_DIM // 2, 1) * sinb)

        def rope_r(xb):
            return xb * cosr + pltpu.roll(xb, RET_HEAD_DIM // 2, 1) * sinr

        def proj(section):
            if section == 0:
                return _dot(un, wq_ref[...])
            if section == 1:
                return _dot(un, wkv_ref[...])
            return _dot(un, wret_ref[:, (section - 2) * PROJ_CHUNK:(section - 1) * PROJ_CHUNK])

        for c, blk in enumerate(lane_blocks(proj(3))):
            cols = slice(c * LANES, (c + 1) * LANES)
            rk = rope_r(blk) * (RET_HEAD_DIM ** -0.5)
            rk_ref[r, cols] = rk.astype(BF16)
            rkf_ref[r, cols] = (rk * decay_ref[0, r, cols]).astype(BF16)
            rkb_ref[r, cols] = (rk * decay_ref[1, r, cols]).astype(BF16)
        gate_ref[r, :] = _silu(proj(5)).astype(BF16)
        for c, blk in enumerate(lane_blocks(proj(0))):
            q_ref[r, c * LANES:(c + 1) * LANES] = (rope_a(blk) * ATTN_Q_SCALE).astype(BF16)
        for c, blk in enumerate(lane_blocks(proj(2))):
            rq_ref[r, c * LANES:(c + 1) * LANES] = rope_r(blk).astype(BF16)
        kv = lane_blocks(proj(1))
        for c, blk in enumerate(kv[:KV_DUP_WIDTH // LANES]):
            k_ref[r, c * LANES:(c + 1) * LANES] = rope_a(blk).astype(BF16)
        for c, blk in enumerate(kv[KV_DUP_WIDTH // LANES:]):
            v_ref[r, c * LANES:(c + 1) * LANES] = blk.astype(BF16)
        rv_ref[r, :] = proj(4).astype(BF16)


def _const_spec(shape):
    return pl.BlockSpec(shape, lambda *_: (0,) * len(shape), pipeline_mode=pl.Buffered(1))


def _ffn_proj(x2d, seq_len, g1, wg, wu, wd, gm, wq, wkv, wret, tables, ld_f, ld_b):
    n = x2d.shape[0]
    tm = min(TOKEN_TILE, seq_len)
    assert tm % RET_CHUNK == 0
    tiles_per_seq = seq_len // tm
    row = lambda i: (i, 0)
    tab = lambda i: (i % tiles_per_seq, 0)
    out_widths = (ATTN_WIDTH, KV_DUP_WIDTH, KV_DUP_WIDTH) + (RET_WIDTH,) * 6
    return pl.pallas_call(
        _ffn_proj_kernel,
        grid=(n // tm,),
        in_specs=[
            pl.BlockSpec(memory_space=pltpu.SMEM),
            pl.BlockSpec(memory_space=pltpu.SMEM),
            pl.BlockSpec((tm, D_MODEL), row),
            _const_spec((1, D_MODEL)),
            _const_spec((D_MODEL, D_FF)),
            _const_spec((D_MODEL, D_FF)),
            _const_spec((D_FF, D_MODEL)),
            _const_spec((1, D_MODEL)),
            _const_spec(wq.shape),
            _const_spec(wkv.shape),
            _const_spec(wret.shape),
        ] + [pl.BlockSpec((tm, LANES), tab)] * 5,
        out_specs=[pl.BlockSpec((tm, D_MODEL), row)] + [pl.BlockSpec((tm, w), row) for w in out_widths],
        out_shape=[jax.ShapeDtypeStruct((n, D_MODEL), F32)]
                  + [jax.ShapeDtypeStruct((n, w), BF16) for w in out_widths],
        scratch_shapes=[pltpu.VMEM((tm, D_FF), BF16),
                        pltpu.VMEM((2, tm, RET_WIDTH), F32)],
        compiler_params=pltpu.CompilerParams(dimension_semantics=("arbitrary",),
                                             vmem_limit_bytes=VMEM_LIMIT),
        name="ffn_proj",
    )(ld_f, ld_b, x2d, g1, wg, wu, wd, gm, wq, wkv, wret, *tables)


def _attn_kernel(sink_ref, q_ref, kp_ref, km_ref, kn_ref, vp_ref, vm_ref, vn_ref, gain_ref,
                 o_ref, klo_ref, khi_ref, vw_ref, bias_ref, s_ref, p_ref, r_ref, acc_ref):
    t = pl.program_id(1)
    nt = pl.num_programs(1)
    ts = q_ref.shape[1]
    nblk = ts // ATTN_BLOCK
    wk = 3 * ATTN_BLOCK

    @pl.when(t == 0)
    def _():
        qi = lax.broadcasted_iota(jnp.int32, (ATTN_BLOCK, wk), 0)
        kj = lax.broadcasted_iota(jnp.int32, (ATTN_BLOCK, wk), 1)
        band = (kj >= qi) & (kj <= qi + 2 * WINDOW)
        bias_ref[0] = jnp.where(band, 0.0, NEG_BIG)
        bias_ref[1] = jnp.where(band & (kj >= ATTN_BLOCK), 0.0, NEG_BIG)
        bias_ref[2] = jnp.where(band & (kj < 2 * ATTN_BLOCK), 0.0, NEG_BIG)

    lo_lanes = (lax.broadcasted_iota(jnp.int32, (1, KV_DUP_WIDTH), 1) % LANES) < ATTN_HEAD_DIM
    for dst, src in ((slice(0, ATTN_BLOCK), kp_ref), (slice(ATTN_BLOCK, ATTN_BLOCK + ts), km_ref),
                     (slice(ATTN_BLOCK + ts, 2 * ATTN_BLOCK + ts), kn_ref)):
        kblk = src[0]
        klo_ref[dst, :] = jnp.where(lo_lanes, kblk, jnp.zeros_like(kblk))
        khi_ref[dst, :] = jnp.where(lo_lanes, jnp.zeros_like(kblk), kblk)
    vw_ref[0:ATTN_BLOCK, :] = vp_ref[0]
    vw_ref[ATTN_BLOCK:ATTN_BLOCK + ts, :] = vm_ref[0]
    vw_ref[ATTN_BLOCK + ts:, :] = vn_ref[0]

    out_lo = lax.broadcasted_iota(jnp.int32, (ATTN_BLOCK, LANES), 1) < ATTN_HEAD_DIM
    first_idx = jnp.where(t == 0, 1, 0)
    last_idx = jnp.where(t == nt - 1, 2, 0)

    units = [(j, g, pair, half) for j in range(nblk) for g in range(N_KV_HEADS)
             for pair in range(2) for half in range(2)]

    def rows_of(j):
        return slice(j * ATTN_BLOCK, (j + 1) * ATTN_BLOCK)

    def win_rows(j):
        return slice(j * ATTN_BLOCK, j * ATTN_BLOCK + wk)

    def pair_cols(g, pair):
        return slice((2 * g + pair) * LANES, (2 * g + pair + 1) * LANES)

    for u, (j, g, pair, half) in enumerate(units):
        k_ref = khi_ref if half else klo_ref
        bias_idx = first_idx if j == 0 else (last_idx if j == nblk - 1 else 0)
        s = _dot_nt(q_ref[0, rows_of(j), pair_cols(g, pair)],
                    k_ref[win_rows(j), g * LANES:(g + 1) * LANES])
        for blk in range(3):
            cols = slice(blk * ATTN_BLOCK, (blk + 1) * ATTN_BLOCK)
            s_ref[u, :, cols] = s[:, cols] if blk == 1 else s[:, cols] + bias_ref[bias_idx, :, cols]

    for u, (j, g, pair, half) in enumerate(units):
        s = s_ref[u]
        sink = sink_ref[4 * g + 2 * pair + half] * LOG2_E
        m = jnp.maximum(jnp.max(s, axis=-1, keepdims=True), sink)
        p = jnp.exp2(s - m)
        denom = jnp.sum(p, axis=-1, keepdims=True) + jnp.exp2(sink - m)
        p_ref[u] = p.astype(BF16)
        r_ref[u] = jnp.broadcast_to(1.0 / denom, (ATTN_BLOCK, LANES))

    pending = []
    for u, (j, g, pair, half) in enumerate(units):
        out = _dot(p_ref[u], vw_ref[win_rows(j), g * LANES:(g + 1) * LANES]) * r_ref[u]
        if half == 0:
            pending.append(out)
        else:
            acc_ref[rows_of(j), pair_cols(g, pair)] = jnp.where(out_lo, pending.pop(), out)

    o_ref[0] = _rmsnorm(acc_ref[...], gain_ref[...]).astype(BF16)


def _attn(q, k, v, sink, gain):
    b, s, _ = q.shape
    ts = min(SEQ_TILE, s)
    nt = s // ts
    bps = ts // ATTN_BLOCK
    nb = s // ATTN_BLOCK
    main = lambda i, t: (i, t, 0)
    prev = lambda i, t: (i, jnp.maximum(t * bps - 1, 0), 0)
    nxt = lambda i, t: (i, jnp.minimum((t + 1) * bps, nb - 1), 0)
    kv_specs = [pl.BlockSpec((1, ATTN_BLOCK, KV_DUP_WIDTH), prev),
                pl.BlockSpec((1, ts, KV_DUP_WIDTH), main),
                pl.BlockSpec((1, ATTN_BLOCK, KV_DUP_WIDTH), nxt)]
    return pl.pallas_call(
        _attn_kernel,
        grid=(b, nt),
        in_specs=[pl.BlockSpec(memory_space=pltpu.SMEM),
                  pl.BlockSpec((1, ts, ATTN_WIDTH), main)] + kv_specs + kv_specs
                 + [pl.BlockSpec((1, ATTN_WIDTH), lambda i, t: (0, 0))],
        out_specs=pl.BlockSpec((1, ts, ATTN_WIDTH), main),
        out_shape=jax.ShapeDtypeStruct((b, s, ATTN_WIDTH), BF16),
        scratch_shapes=[pltpu.VMEM((ts + 2 * ATTN_BLOCK, KV_DUP_WIDTH), BF16),
                        pltpu.VMEM((ts + 2 * ATTN_BLOCK, KV_DUP_WIDTH), BF16),
                        pltpu.VMEM((ts + 2 * ATTN_BLOCK, KV_DUP_WIDTH), BF16),
                        pltpu.VMEM((3, ATTN_BLOCK, 3 * ATTN_BLOCK), F32),
                        pltpu.VMEM((bps * N_ATTN_HEADS, ATTN_BLOCK, 3 * ATTN_BLOCK), F32),
                        pltpu.VMEM((bps * N_ATTN_HEADS, ATTN_BLOCK, 3 * ATTN_BLOCK), BF16),
                        pltpu.VMEM((bps * N_ATTN_HEADS, ATTN_BLOCK, LANES), F32),
                        pltpu.VMEM((ts, ATTN_WIDTH), F32)],
        compiler_params=pltpu.CompilerParams(dimension_semantics=("parallel", "arbitrary"),
                                             vmem_limit_bytes=VMEM_LIMIT),
        name="attn",
    )(sink, q, k, k, k, v, v, v, gain)


def _head_lane_values(ld_ref):
    lane = lax.broadcasted_iota(jnp.int32, (1, RET_WIDTH), 1)
    ld = jnp.zeros((1, RET_WIDTH), F32)
    for h in range(N_RET_HEADS):
        ld = jnp.where(lane >= h * RET_HEAD_DIM, ld_ref[h], ld)
    return ld


def _chunk_decay(ld_ref):
    return jnp.exp(_head_lane_values(ld_ref) * float(RET_CHUNK))


def _head_cols(h):
    return slice(h * RET_HEAD_DIM, (h + 1) * RET_HEAD_DIM)


def _chunk_rows(c):
    return slice(c * RET_CHUNK, (c + 1) * RET_CHUNK)


def _ret_state_kernel(ldb_ref, rkb_ref, rv_ref, sb_ref, state_ref, kv_ref):
    t = pl.program_id(1)
    nchunk = rkb_ref.shape[1] // RET_CHUNK

    @pl.when(t == 0)
    def _():
        state_ref[...] = jnp.zeros_like(state_ref)

    for c in range(nchunk):
        for h in range(N_RET_HEADS):
            kv_ref[c, :, _head_cols(h)] = _dot_tn(rkb_ref[0, _chunk_rows(c), _head_cols(h)],
                                                  rv_ref[0, _chunk_rows(c), _head_cols(h)])

    chunk_decay = _chunk_decay(ldb_ref)
    state = state_ref[...]
    for c in reversed(range(nchunk)):
        sb_ref[0, c] = state.astype(BF16)
        state = state * chunk_decay + kv_ref[c]
    state_ref[...] = state


def _ret_state(rkb, rv, ld_b):
    b, s, _ = rkb.shape
    ts = min(RET_STATE_TILE, s)
    nt = s // ts
    cpt = ts // RET_CHUNK
    rev = lambda i, t: (i, nt - 1 - t, 0)
    return pl.pallas_call(
        _ret_state_kernel,
        grid=(b, nt),
        in_specs=[pl.BlockSpec(memory_space=pltpu.SMEM),
                  pl.BlockSpec((1, ts, RET_WIDTH), rev),
                  pl.BlockSpec((1, ts, RET_WIDTH), rev)],
        out_specs=pl.BlockSpec((1, cpt, RET_HEAD_DIM, RET_WIDTH), lambda i, t: (i, nt - 1 - t, 0, 0)),
        out_shape=jax.ShapeDtypeStruct((b, s // RET_CHUNK, RET_HEAD_DIM, RET_WIDTH), BF16),
        scratch_shapes=[pltpu.VMEM((RET_HEAD_DIM, RET_WIDTH), F32),
                        pltpu.VMEM((cpt, RET_HEAD_DIM, RET_WIDTH), F32)],
        compiler_params=pltpu.CompilerParams(dimension_semantics=("parallel", "arbitrary"),
                                             vmem_limit_bytes=VMEM_LIMIT),
        name="ret_state",
    )(ld_b, rkb, rv)


def _ret_out_kernel(ldf_ref, ldb_ref, rq_ref, rk_ref, rkf_ref, rv_ref, gate_ref, sb_ref, o_ref,
                    state_ref, tab_ref, inner_ref, kv_ref, cs_ref):
    t = pl.program_id(1)
    nchunk = rq_ref.shape[1] // RET_CHUNK
    MASK, Q_FWD, Q_BWD = range(3)

    @pl.when(t == 0)
    def _():
        state_ref[...] = jnp.zeros_like(state_ref)
        shape = (RET_CHUNK, RET_HEAD_DIM)
        row_i = lax.broadcasted_iota(jnp.int32, shape, 0)
        col_i = lax.broadcasted_iota(jnp.int32, shape, 1)
        row = row_i.astype(F32)
        diff = (row_i - col_i).astype(F32)
        for h in range(N_RET_HEADS):
            d_fwd = jnp.where(diff >= 0, jnp.exp(ldf_ref[h] * jnp.maximum(diff, 0.0)), 0.0)
            d_bwd = jnp.where(diff <= 0, jnp.exp(ldb_ref[h] * jnp.maximum(-diff, 0.0)), 0.0)
            tab_ref[MASK, h] = d_fwd + d_bwd
            tab_ref[Q_FWD, h] = jnp.exp(ldf_ref[h] * (row + 1.0))
            tab_ref[Q_BWD, h] = jnp.exp(ldb_ref[h] * (float(RET_CHUNK) - row))

    units = [(c, h) for c in range(nchunk) for h in range(N_RET_HEADS)]

    for u, (c, h) in enumerate(units):
        q = rq_ref[0, _chunk_rows(c), _head_cols(h)]
        k = rk_ref[0, _chunk_rows(c), _head_cols(h)]
        inner_ref[u] = (_dot_nt(q, k) * tab_ref[MASK, h]).astype(BF16)
        kv_ref[c, :, _head_cols(h)] = _dot_tn(rkf_ref[0, _chunk_rows(c), _head_cols(h)],
                                              rv_ref[0, _chunk_rows(c), _head_cols(h)])

    chunk_decay = _chunk_decay(ldf_ref)
    state = state_ref[...]
    for c in range(nchunk):
        fwd = state.astype(BF16)
        for h in range(N_RET_HEADS):
            base = 2 * h * RET_HEAD_DIM
            cs_ref[c, :, base:base + RET_HEAD_DIM] = fwd[:, _head_cols(h)]
            cs_ref[c, :, base + RET_HEAD_DIM:base + 2 * RET_HEAD_DIM] = sb_ref[0, c, :, _head_cols(h)]
        state = state * chunk_decay + kv_ref[c]
    state_ref[...] = state

    for u, (c, h) in enumerate(units):
        q = rq_ref[0, _chunk_rows(c), _head_cols(h)]
        o = _dot(inner_ref[u], rv_ref[0, _chunk_rows(c), _head_cols(h)])
        cross = _dot(q, cs_ref[c, :, 2 * h * RET_HEAD_DIM:2 * (h + 1) * RET_HEAD_DIM])
        o += tab_ref[Q_FWD, h] * cross[:, :RET_HEAD_DIM] + tab_ref[Q_BWD, h] * cross[:, RET_HEAD_DIM:]
        mu = jnp.mean(o, axis=-1, keepdims=True)
        var = jnp.mean(jnp.square(o - mu), axis=-1, keepdims=True)
        on = (o - mu) * lax.rsqrt(var + EPS)
        gate = gate_ref[0, _chunk_rows(c), _head_cols(h)].astype(F32)
        o_ref[0, _chunk_rows(c), _head_cols(h)] = (gate * on).astype(BF16)


def _ret_out(rq, rk, rkf, rv, gate, sb, ld_f, ld_b):
    b, s, _ = rq.shape
    ts = min(SEQ_TILE, s)
    nt = s // ts
    cpt = ts // RET_CHUNK
    main = lambda i, t: (i, t, 0)
    seq_spec = pl.BlockSpec((1, ts, RET_WIDTH), main)
    return pl.pallas_call(
        _ret_out_kernel,
        grid=(b, nt),
        in_specs=[pl.BlockSpec(memory_space=pltpu.SMEM), pl.BlockSpec(memory_space=pltpu.SMEM),
                  seq_spec, seq_spec, seq_spec, seq_spec, seq_spec,
                  pl.BlockSpec((1, cpt, RET_HEAD_DIM, RET_WIDTH), lambda i, t: (i, t, 0, 0))],
        out_specs=seq_spec,
        out_shape=jax.ShapeDtypeStruct((b, s, RET_WIDTH), BF16),
        scratch_shapes=[pltpu.VMEM((RET_HEAD_DIM, RET_WIDTH), F32),
                        pltpu.VMEM((3, N_RET_HEADS, RET_CHUNK, RET_HEAD_DIM), F32),
                        pltpu.VMEM((cpt * N_RET_HEADS, RET_CHUNK, RET_CHUNK), BF16),
                        pltpu.VMEM((cpt, RET_HEAD_DIM, RET_WIDTH), F32),
                        pltpu.VMEM((cpt, RET_HEAD_DIM, 2 * RET_WIDTH), BF16)],
        compiler_params=pltpu.CompilerParams(dimension_semantics=("parallel", "arbitrary"),
                                             vmem_limit_bytes=VMEM_LIMIT),
        name="ret_out",
    )(ld_f, ld_b, rq, rk, rkf, rv, gate, sb)


def _out_ffn_kernel(h_ref, a_ref, r_ref, woa_ref, wor_ref, g2_ref, wg_ref, wu_ref, wd_ref, gf_ref,
                    y_ref, act_ref):
    groups = _row_groups(h_ref.shape[0])
    hs = [h_ref[r, :] + _dot(a_ref[r, :], woa_ref[...]) + _dot(r_ref[r, :], wor_ref[...]) for r in groups]
    hns = [_rmsnorm(h, g2_ref[...]).astype(BF16) for h in hs]
    for r, hn in zip(groups, hns):
        _gate_up(hn, r, wg_ref, wu_ref, act_ref)
    ys = [_dot(act_ref[r, :], wd_ref[...]) for r in groups]
    for r, h, y in zip(groups, hs, ys):
        y_ref[r, :] = _rmsnorm(h + 0.5 * y, gf_ref[...])


def _out_ffn(h, a, r, woa, wor, g2, wg, wu, wd, gf):
    n = h.shape[0]
    tm = min(OUT_TOKEN_TILE, n)
    row = lambda i: (i, 0)
    return pl.pallas_call(
        _out_ffn_kernel,
        grid=(n // tm,),
        in_specs=[
            pl.BlockSpec((tm, D_MODEL), row),
            pl.BlockSpec((tm, ATTN_WIDTH), row),
            pl.BlockSpec((tm, RET_WIDTH), row),
            _const_spec((ATTN_WIDTH, D_MODEL)),
            _const_spec((RET_WIDTH, D_MODEL)),
            _const_spec((1, D_MODEL)),
            _const_spec((D_MODEL, D_FF)),
            _const_spec((D_MODEL, D_FF)),
            _const_spec((D_FF, D_MODEL)),
            _const_spec((1, D_MODEL)),
        ],
        out_specs=pl.BlockSpec((tm, D_MODEL), row),
        out_shape=jax.ShapeDtypeStruct((n, D_MODEL), F32),
        scratch_shapes=[pltpu.VMEM((tm, D_FF), BF16)],
        compiler_params=pltpu.CompilerParams(dimension_semantics=("parallel",),
                                             vmem_limit_bytes=VMEM_LIMIT),
        name="out_ffn",
    )(h, a, r, woa, wor, g2, wg, wu, wd, gf)


def _rope_tables(seq_len):
    half = ROT_DIM // 2
    lane = jnp.arange(LANES) % ATTN_HEAD_DIM
    inv_a = ROPE_THETA ** (-jnp.arange(0, ROT_DIM, 2, dtype=F32) / ROT_DIM)
    freq_a = jnp.where(lane < ROT_DIM, inv_a[lane % half], 0.0)
    inv_r = RET_ROT_THETA ** (-jnp.linspace(0.0, 1.0, RET_HEAD_DIM // 2, dtype=F32))
    freq_r = jnp.concatenate([inv_r, inv_r])

    start = (jnp.arange(seq_len // ROPE_BLOCK, dtype=F32) * ROPE_BLOCK)[:, None, None]
    offset = jnp.arange(ROPE_BLOCK, dtype=F32)[None, :, None]

    def cos_sin(freq):
        a, b = start * freq, offset * freq
        cos = jnp.cos(a) * jnp.cos(b) - jnp.sin(a) * jnp.sin(b)
        sin = jnp.sin(a) * jnp.cos(b) + jnp.cos(a) * jnp.sin(b)
        return cos.reshape(seq_len, LANES), sin.reshape(seq_len, LANES)

    cos_a, sin = cos_sin(freq_a)
    sin_a = sin * jnp.where(lane < half, -1.0, 0.0)[None, :]
    sin_b = sin * jnp.where((lane >= half) & (lane < ROT_DIM), 1.0, 0.0)[None, :]
    cos_r, sin = cos_sin(freq_r)
    sign_r = jnp.where(jnp.arange(LANES) < RET_HEAD_DIM // 2, -1.0, 1.0)
    return cos_a, sin_a, sin_b, cos_r, sin * sign_r[None, :]


def _dup_kv_columns(w):
    heads = [w[:, h * ATTN_HEAD_DIM:(h + 1) * ATTN_HEAD_DIM] for h in range(N_KV_HEADS)]
    return jnp.concatenate([heads[0], heads[0], heads[1], heads[1]], axis=1)


def _layer(x, p, tables):
    b, s, _ = x.shape
    n = b * s
    h, q, k, v, rq, rk, rkf, rkb, rv, gate = _ffn_proj(
        x.reshape(n, D_MODEL), s, p["g1"], p["wg1"], p["wu1"], p["wd1"], p["gm"],
        p["wq"], p["wkv"], p["wret"], tables,
        p["ldf"], p["ldb"])
    seq = lambda a: a.reshape(b, s, a.shape[-1])
    attn = _attn(seq(q), seq(k), seq(v), p["sink"], p["ga"])
    sb = _ret_state(seq(rkb), seq(rv), p["ldb"])
    ret = _ret_out(seq(rq), seq(rk), seq(rkf), seq(rv), seq(gate), sb, p["ldf"], p["ldb"])
    y = _out_ffn(h, attn.reshape(n, ATTN_WIDTH), ret.reshape(n, RET_WIDTH),
                 p["woa"], p["wor"], p["g2"], p["wg2"], p["wu2"], p["wd2"], p["gf"])
    return y.reshape(b, s, D_MODEL)


def kernel(x_prompt, x_sample, ffn1_norm, ffn1_w_gate, ffn1_w_up, ffn1_w_down, mix_norm, w_in, attn_sink, attn_out_norm, ret_log_decay_fwd, ret_log_decay_bwd, w_out, ffn2_norm, ffn2_w_gate, ffn2_w_up, ffn2_w_down, final_norm):
    assert ffn1_norm.shape[0] == 1
    o1 = ATTN_WIDTH
    o2 = o1 + N_KV_HEADS * ATTN_HEAD_DIM
    o3 = o2 + N_KV_HEADS * ATTN_HEAD_DIM
    wi = w_in[0]
    wkv = jnp.concatenate([_dup_kv_columns(wi[:, o1:o2].astype(BF16)),
                           _dup_kv_columns(wi[:, o2:o3].astype(BF16))], axis=1)
    wo = w_out[0].astype(BF16)
    p = dict(
        g1=ffn1_norm[0][None, :], wg1=ffn1_w_gate[0].astype(BF16), wu1=ffn1_w_up[0].astype(BF16),
        wd1=ffn1_w_down[0].astype(BF16), gm=mix_norm[0][None, :],
        wq=wi[:, :o1].astype(BF16), wkv=wkv, wret=wi[:, o3:].astype(BF16),
        sink=attn_sink[0], ga=attn_out_norm[0][None, :],
        ldf=ret_log_decay_fwd[0], ldb=ret_log_decay_bwd[0],
        woa=wo[:ATTN_WIDTH], wor=wo[ATTN_WIDTH:],
        g2=ffn2_norm[0][None, :], wg2=ffn2_w_gate[0].astype(BF16), wu2=ffn2_w_up[0].astype(BF16),
        wd2=ffn2_w_down[0].astype(BF16), gf=final_norm[None, :],
    )
    tables = _rope_tables(max(x_prompt.shape[1], x_sample.shape[1]))
    return (_layer(x_prompt, p, tables), _layer(x_sample, p, tables))
```

```python
import functools

import jax
import jax.numpy as jnp
from jax import lax
from jax.experimental import pallas as pl
from jax.experimental.pallas import tpu as pltpu

D_MODEL = 1024
D_FF = 2816
N_ATTN_HEADS = 8
N_KV_HEADS = 2
ATTN_HEAD_DIM = 64
ATTN_WIDTH = N_ATTN_HEADS * ATTN_HEAD_DIM
WINDOW = 128
ATTN_BLOCK = 128
ROT_DIM = 16
ROPE_THETA = 500000.0
RET_HEAD_DIM = 128
N_RET_HEADS = 4
RET_WIDTH = N_RET_HEADS * RET_HEAD_DIM
RET_CHUNK = 128
RET_ROT_THETA = 10000.0
EPS = 1e-6
NEG_BIG = -1e30
LOG2_E = 1.4426950408889634
ATTN_Q_SCALE = ATTN_HEAD_DIM ** -0.5 * LOG2_E

LANES = 128
KV_DUP_WIDTH = 2 * LANES
FF_CHUNK = 256
PROJ_CHUNK = 512
ROW_GROUP = 256
TOKEN_TILE = 512
OUT_TOKEN_TILE = 1024
SEQ_TILE = 1024
RET_STATE_TILE = 2048
ROPE_BLOCK = 128
VMEM_LIMIT = 56 * 1024 * 1024

BF16 = jnp.bfloat16
F32 = jnp.float32


def _dot(a, b):
    return jnp.dot(a, b, preferred_element_type=F32)


def _dot_nt(a, b):
    return lax.dot_general(a, b, (((1,), (1,)), ((), ())), preferred_element_type=F32)


def _dot_tn(a, b):
    return lax.dot_general(a, b, (((0,), (0,)), ((), ())), preferred_element_type=F32)


def _rmsnorm(x, g):
    return x * lax.rsqrt(jnp.mean(x * x, axis=-1, keepdims=True) + EPS) * g


def _silu(x):
    return x * (1.0 / (1.0 + jnp.exp(-x)))


def _row_groups(rows):
    return [slice(i, i + ROW_GROUP) for i in range(0, rows, ROW_GROUP)]


def _gate_up(xn, rows, wg_ref, wu_ref, act_ref):
    for c in range(D_FF // FF_CHUNK):
        cols = slice(c * FF_CHUNK, (c + 1) * FF_CHUNK)
        g = _dot(xn, wg_ref[:, cols])
        u = _dot(xn, wu_ref[:, cols])
        act_ref[rows, cols] = (_silu(g) * u).astype(BF16)


def _ffn_proj_kernel(ldf_ref, ldb_ref, x_ref, g1_ref, wg_ref, wu_ref, wd_ref, gm_ref,
                     win_ref,
                     rope_ref,
                     h_ref, q_ref, k_ref, v_ref, rq_ref, rk_ref, rkf_ref, rkb_ref, rv_ref, gate_ref,
                     act_ref, decay_ref):
    @pl.when(pl.program_id(0) == 0)
    def _():
        shape = decay_ref.shape[1:]
        j = (lax.broadcasted_iota(jnp.int32, shape, 0) % RET_CHUNK).astype(F32)
        decay_ref[0] = jnp.exp(_head_lane_values(ldf_ref) * (float(RET_CHUNK) - 1.0 - j))
        decay_ref[1] = jnp.exp(_head_lane_values(ldb_ref) * j)

    groups = _row_groups(x_ref.shape[0])
    xns = [_rmsnorm(x_ref[r, :], g1_ref[...]).astype(BF16) for r in groups]
    for r, xn in zip(groups, xns):
        _gate_up(xn, r, wg_ref, wu_ref, act_ref)
    ys = [_dot(act_ref[r, :], wd_ref[...]) for r in groups]

    def lane_blocks(p):
        return [p[:, c * LANES:(c + 1) * LANES] for c in range(p.shape[1] // LANES)]

    for r, y in zip(groups, ys):
        h = x_ref[r, :] + 0.5 * y
        h_ref[r, :] = h
        un = _rmsnorm(h, gm_ref[...]).astype(BF16)

        cosa, sina, sinb, cosr, sinr = (rope_ref[r, t * LANES:(t + 1) * LANES] for t in range(5))

        def rope_a(xb):
            return (xb * cosa + pltpu.roll(xb, LANES - ROT_DIM // 2, 1) * sina
                    + pltpu.roll(xb, ROT_DIM // 2, 1) * sinb)

        def rope_r(xb):
            return xb * cosr + pltpu.roll(xb, RET_HEAD_DIM // 2, 1) * sinr

        def proj(section):
            kv_cols = 2 * N_KV_HEADS * ATTN_HEAD_DIM
            if section == 0:
                cols = slice(0, ATTN_WIDTH)
            elif section == 1:
                cols = slice(ATTN_WIDTH, ATTN_WIDTH + kv_cols)
            else:
                start = ATTN_WIDTH + kv_cols + (section - 2) * PROJ_CHUNK
                cols = slice(start, start + PROJ_CHUNK)
            return _dot(un, win_ref[:, cols])

        for c, blk in enumerate(lane_blocks(proj(3))):
            cols = slice(c * LANES, (c + 1) * LANES)
            rk = rope_r(blk) * (RET_HEAD_DIM ** -0.5)
            rk_ref[r, cols] = rk.astype(BF16)
            rkf_ref[r, cols] = (rk * decay_ref[0, r, cols]).astype(BF16)
            rkb_ref[r, cols] = (rk * decay_ref[1, r, cols]).astype(BF16)
        gate_ref[r, :] = _silu(proj(5)).astype(BF16)
        for c, blk in enumerate(lane_blocks(proj(0))):
            q_ref[r, c * LANES:(c + 1) * LANES] = (rope_a(blk) * ATTN_Q_SCALE).astype(BF16)
        for c, blk in enumerate(lane_blocks(proj(2))):
            rq_ref[r, c * LANES:(c + 1) * LANES] = rope_r(blk).astype(BF16)
        kblk, vblk = lane_blocks(proj(1))
        lo_lanes = lax.broadcasted_iota(jnp.int32, kblk.shape, 1) < ATTN_HEAD_DIM
        for ref, blk in ((k_ref, rope_a(kblk)), (v_ref, vblk)):
            swapped = pltpu.roll(blk, ATTN_HEAD_DIM, 1)
            ref[r, 0:LANES] = jnp.where(lo_lanes, blk, swapped).astype(BF16)
            ref[r, LANES:2 * LANES] = jnp.where(lo_lanes, swapped, blk).astype(BF16)
        rv_ref[r, :] = proj(4).astype(BF16)


def _const_spec(shape):
    return pl.BlockSpec(shape, lambda *_: (0,) * len(shape), pipeline_mode=pl.Buffered(1))


def _ffn_proj(x2d, seq_len, g1, wg, wu, wd, gm, win, tables, ld_f, ld_b):
    n = x2d.shape[0]
    tm = min(TOKEN_TILE, seq_len)
    assert tm % RET_CHUNK == 0
    tiles_per_seq = seq_len // tm
    row = lambda i: (i, 0)
    tab = lambda i: (i % tiles_per_seq, 0)
    out_widths = (ATTN_WIDTH, KV_DUP_WIDTH, KV_DUP_WIDTH) + (RET_WIDTH,) * 6
    return pl.pallas_call(
        _ffn_proj_kernel,
        grid=(n // tm,),
        in_specs=[
            pl.BlockSpec(memory_space=pltpu.SMEM),
            pl.BlockSpec(memory_space=pltpu.SMEM),
            pl.BlockSpec((tm, D_MODEL), row),
            _const_spec((1, D_MODEL)),
            _const_spec((D_MODEL, D_FF)),
            _const_spec((D_MODEL, D_FF)),
            _const_spec((D_FF, D_MODEL)),
            _const_spec((1, D_MODEL)),
            _const_spec(win.shape),
            pl.BlockSpec((tm, tables.shape[1]), tab),
        ],
        out_specs=[pl.BlockSpec((tm, D_MODEL), row)] + [pl.BlockSpec((tm, w), row) for w in out_widths],
        out_shape=[jax.ShapeDtypeStruct((n, D_MODEL), F32)]
                  + [jax.ShapeDtypeStruct((n, w), BF16) for w in out_widths],
        scratch_shapes=[pltpu.VMEM((tm, D_FF), BF16),
                        pltpu.VMEM((2, tm, RET_WIDTH), F32)],
        compiler_params=pltpu.CompilerParams(dimension_semantics=("arbitrary",),
                                             vmem_limit_bytes=VMEM_LIMIT),
        name="ffn_proj",
    )(ld_f, ld_b, x2d, g1, wg, wu, wd, gm, win, tables)


def _attn_kernel(sink_ref, q_ref, kp_ref, km_ref, kn_ref, vp_ref, vm_ref, vn_ref, gain_ref,
                 o_ref, klo_ref, khi_ref, vw_ref, bias_ref, s_ref, p_ref, r_ref, acc_ref):
    t = pl.program_id(1)
    nt = pl.num_programs(1)
    ts = q_ref.shape[1]
    nblk = ts // ATTN_BLOCK
    wk = 3 * ATTN_BLOCK

    @pl.when(t == 0)
    def _():
        qi = lax.broadcasted_iota(jnp.int32, (ATTN_BLOCK, wk), 0)
        kj = lax.broadcasted_iota(jnp.int32, (ATTN_BLOCK, wk), 1)
        band = (kj >= qi) & (kj <= qi + 2 * WINDOW)
        bias_ref[0] = jnp.where(band, 0.0, NEG_BIG)
        bias_ref[1] = jnp.where(band & (kj >= ATTN_BLOCK), 0.0, NEG_BIG)
        bias_ref[2] = jnp.where(band & (kj < 2 * ATTN_BLOCK), 0.0, NEG_BIG)

    lo_lanes = (lax.broadcasted_iota(jnp.int32, (1, KV_DUP_WIDTH), 1) % LANES) < ATTN_HEAD_DIM
    for dst, src in ((slice(0, ATTN_BLOCK), kp_ref), (slice(ATTN_BLOCK, ATTN_BLOCK + ts), km_ref),
                     (slice(ATTN_BLOCK + ts, 2 * ATTN_BLOCK + ts), kn_ref)):
        kblk = src[0]
        klo_ref[dst, :] = jnp.where(lo_lanes, kblk, jnp.zeros_like(kblk))
        khi_ref[dst, :] = jnp.where(lo_lanes, jnp.zeros_like(kblk), kblk)
    vw_ref[0:ATTN_BLOCK, :] = vp_ref[0]
    vw_ref[ATTN_BLOCK:ATTN_BLOCK + ts, :] = vm_ref[0]
    vw_ref[ATTN_BLOCK + ts:, :] = vn_ref[0]

    out_lo = lax.broadcasted_iota(jnp.int32, (ATTN_BLOCK, LANES), 1) < ATTN_HEAD_DIM
    first_idx = jnp.where(t == 0, 1, 0)
    last_idx = jnp.where(t == nt - 1, 2, 0)

    units = [(j, g, pair, half) for j in range(nblk) for g in range(N_KV_HEADS)
             for pair in range(2) for half in range(2)]

    def rows_of(j):
        return slice(j * ATTN_BLOCK, (j + 1) * ATTN_BLOCK)

    def win_rows(j):
        return slice(j * ATTN_BLOCK, j * ATTN_BLOCK + wk)

    def pair_cols(g, pair):
        return slice((2 * g + pair) * LANES, (2 * g + pair + 1) * LANES)

    for u, (j, g, pair, half) in enumerate(units):
        k_ref = khi_ref if half else klo_ref
        bias_idx = first_idx if j == 0 else (last_idx if j == nblk - 1 else 0)
        s = _dot_nt(q_ref[0, rows_of(j), pair_cols(g, pair)],
                    k_ref[win_rows(j), g * LANES:(g + 1) * LANES])
        for blk in range(3):
            cols = slice(blk * ATTN_BLOCK, (blk + 1) * ATTN_BLOCK)
            s_ref[u, :, cols] = s[:, cols] if blk == 1 else s[:, cols] + bias_ref[bias_idx, :, cols]

    for u, (j, g, pair, half) in enumerate(units):
        s = s_ref[u]
        sink = sink_ref[4 * g + 2 * pair + half] * LOG2_E
        m = jnp.maximum(jnp.max(s, axis=-1, keepdims=True), sink)
        p = jnp.exp2(s - m)
        denom = jnp.sum(p, axis=-1, keepdims=True) + jnp.exp2(sink - m)
        p_ref[u] = p.astype(BF16)
        r_ref[u] = jnp.broadcast_to(1.0 / denom, (ATTN_BLOCK, LANES))

    pending = []
    for u, (j, g, pair, half) in enumerate(units):
        out = _dot(p_ref[u], vw_ref[win_rows(j), g * LANES:(g + 1) * LANES]) * r_ref[u]
        if half == 0:
            pending.append(out)
        else:
            acc_ref[rows_of(j), pair_cols(g, pair)] = jnp.where(out_lo, pending.pop(), out)

    o_ref[0] = _rmsnorm(acc_ref[...], gain_ref[...]).astype(BF16)


def _attn(q, k, v, sink, gain):
    b, s, _ = q.shape
    ts = min(SEQ_TILE, s)
    nt = s // ts
    bps = ts // ATTN_BLOCK
    nb = s // ATTN_BLOCK
    main = lambda i, t: (i, t, 0)
    prev = lambda i, t: (i, jnp.maximum(t * bps - 1, 0), 0)
    nxt = lambda i, t: (i, jnp.minimum((t + 1) * bps, nb - 1), 0)
    kv_specs = [pl.BlockSpec((1, ATTN_BLOCK, KV_DUP_WIDTH), prev),
                pl.BlockSpec((1, ts, KV_DUP_WIDTH), main),
                pl.BlockSpec((1, ATTN_BLOCK, KV_DUP_WIDTH), nxt)]
    return pl.pallas_call(
        _attn_kernel,
        grid=(b, nt),
        in_specs=[pl.BlockSpec(memory_space=pltpu.SMEM),
                  pl.BlockSpec((1, ts, ATTN_WIDTH), main)] + kv_specs + kv_specs
                 + [pl.BlockSpec((1, ATTN_WIDTH), lambda i, t: (0, 0))],
        out_specs=pl.BlockSpec((1, ts, ATTN_WIDTH), main),
        out_shape=jax.ShapeDtypeStruct((b, s, ATTN_WIDTH), BF16),
        scratch_shapes=[pltpu.VMEM((ts + 2 * ATTN_BLOCK, KV_DUP_WIDTH), BF16),
                        pltpu.VMEM((ts + 2 * ATTN_BLOCK, KV_DUP_WIDTH), BF16),
                        pltpu.VMEM((ts + 2 * ATTN_BLOCK, KV_DUP_WIDTH), BF16),
                        pltpu.VMEM((3, ATTN_BLOCK, 3 * ATTN_BLOCK), F32),
                        pltpu.VMEM((bps * N_ATTN_HEADS, ATTN_BLOCK, 3 * ATTN_BLOCK), F32),
                        pltpu.VMEM((bps * N_ATTN_HEADS, ATTN_BLOCK, 3 * ATTN_BLOCK), BF16),
                        pltpu.VMEM((bps * N_ATTN_HEADS, ATTN_BLOCK, LANES), F32),
                        pltpu.VMEM((ts, ATTN_WIDTH), F32)],
        compiler_params=pltpu.CompilerParams(dimension_semantics=("parallel", "arbitrary"),
                                             vmem_limit_bytes=VMEM_LIMIT),
        name="attn",
    )(sink, q, k, k, k, v, v, v, gain)


def _head_lane_values(ld_ref):
    lane = lax.broadcasted_iota(jnp.int32, (1, RET_WIDTH), 1)
    ld = jnp.zeros((1, RET_WIDTH), F32)
    for h in range(N_RET_HEADS):
        ld = jnp.where(lane >= h * RET_HEAD_DIM, ld_ref[h], ld)
    return ld


def _chunk_decay(ld_ref):
    return jnp.exp(_head_lane_values(ld_ref) * float(RET_CHUNK))


def _head_cols(h):
    return slice(h * RET_HEAD_DIM, (h + 1) * RET_HEAD_DIM)


def _chunk_rows(c):
    return slice(c * RET_CHUNK, (c + 1) * RET_CHUNK)


def _ret_state_kernel(ldb_ref, rkb_ref, rv_ref, sb_ref, state_ref, kv_ref):
    t = pl.program_id(1)
    nchunk = rkb_ref.shape[1] // RET_CHUNK

    @pl.when(t == 0)
    def _():
        state_ref[...] = jnp.zeros_like(state_ref)

    for c in range(nchunk):
        for h in range(N_RET_HEADS):
            kv_ref[c, :, _head_cols(h)] = _dot_tn(rkb_ref[0, _chunk_rows(c), _head_cols(h)],
                                                  rv_ref[0, _chunk_rows(c), _head_cols(h)])

    chunk_decay = _chunk_decay(ldb_ref)
    state = state_ref[...]
    for c in reversed(range(nchunk)):
        sb_ref[0, c] = state.astype(BF16)
        state = state * chunk_decay + kv_ref[c]
    state_ref[...] = state


def _ret_state(rkb, rv, ld_b):
    b, s, _ = rkb.shape
    ts = min(RET_STATE_TILE, s)
    nt = s // ts
    cpt = ts // RET_CHUNK
    rev = lambda i, t: (i, nt - 1 - t, 0)
    return pl.pallas_call(
        _ret_state_kernel,
        grid=(b, nt),
        in_specs=[pl.BlockSpec(memory_space=pltpu.SMEM),
                  pl.BlockSpec((1, ts, RET_WIDTH), rev),
                  pl.BlockSpec((1, ts, RET_WIDTH), rev)],
        out_specs=pl.BlockSpec((1, cpt, RET_HEAD_DIM, RET_WIDTH), lambda i, t: (i, nt - 1 - t, 0, 0)),
        out_shape=jax.ShapeDtypeStruct((b, s // RET_CHUNK, RET_HEAD_DIM, RET_WIDTH), BF16),
        scratch_shapes=[pltpu.VMEM((RET_HEAD_DIM, RET_WIDTH), F32),
                        pltpu.VMEM((cpt, RET_HEAD_DIM, RET_WIDTH), F32)],
        compiler_params=pltpu.CompilerParams(dimension_semantics=("parallel", "arbitrary"),
                                             vmem_limit_bytes=VMEM_LIMIT),
        name="ret_state",
    )(ld_b, rkb, rv)


def _ret_out_kernel(ldf_ref, ldb_ref, rq_ref, rk_ref, rkf_ref, rv_ref, gate_ref, sb_ref, o_ref,
                    state_ref, tab_ref, inner_ref, kv_ref, cs_ref):
    t = pl.program_id(1)
    nchunk = rq_ref.shape[1] // RET_CHUNK
    MASK, Q_FWD, Q_BWD = range(3)

    @pl.when(t == 0)
    def _():
        state_ref[...] = jnp.zeros_like(state_ref)
        shape = (RET_CHUNK, RET_HEAD_DIM)
        row_i = lax.broadcasted_iota(jnp.int32, shape, 0)
        col_i = lax.broadcasted_iota(jnp.int32, shape, 1)
        row = row_i.astype(F32)
        diff = (row_i - col_i).astype(F32)
        for h in range(N_RET_HEADS):
            d_fwd = jnp.where(diff >= 0, jnp.exp(ldf_ref[h] * jnp.maximum(diff, 0.0)), 0.0)
            d_bwd = jnp.where(diff <= 0, jnp.exp(ldb_ref[h] * jnp.maximum(-diff, 0.0)), 0.0)
            tab_ref[MASK, h] = d_fwd + d_bwd
            tab_ref[Q_FWD, h] = jnp.exp(ldf_ref[h] * (row + 1.0))
            tab_ref[Q_BWD, h] = jnp.exp(ldb_ref[h] * (float(RET_CHUNK) - row))

    units = [(c, h) for c in range(nchunk) for h in range(N_RET_HEADS)]

    for u, (c, h) in enumerate(units):
        q = rq_ref[0, _chunk_rows(c), _head_cols(h)]
        k = rk_ref[0, _chunk_rows(c), _head_cols(h)]
        inner_ref[u] = (_dot_nt(q, k) * tab_ref[MASK, h]).astype(BF16)
        kv_ref[c, :, _head_cols(h)] = _dot_tn(rkf_ref[0, _chunk_rows(c), _head_cols(h)],
                                              rv_ref[0, _chunk_rows(c), _head_cols(h)])

    chunk_decay = _chunk_decay(ldf_ref)
    state = state_ref[...]
    for c in range(nchunk):
        fwd = state.astype(BF16)
        for h in range(N_RET_HEADS):
            base = 2 * h * RET_HEAD_DIM
            cs_ref[c, :, base:base + RET_HEAD_DIM] = fwd[:, _head_cols(h)]
            cs_ref[c, :, base + RET_HEAD_DIM:base + 2 * RET_HEAD_DIM] = sb_ref[0, c, :, _head_cols(h)]
        state = state * chunk_decay + kv_ref[c]
    state_ref[...] = state

    for u, (c, h) in enumerate(units):
        q = rq_ref[0, _chunk_rows(c), _head_cols(h)]
        o = _dot(inner_ref[u], rv_ref[0, _chunk_rows(c), _head_cols(h)])
        cross = _dot(q, cs_ref[c, :, 2 * h * RET_HEAD_DIM:2 * (h + 1) * RET_HEAD_DIM])
        o += tab_ref[Q_FWD, h] * cross[:, :RET_HEAD_DIM] + tab_ref[Q_BWD, h] * cross[:, RET_HEAD_DIM:]
        mu = jnp.mean(o, axis=-1, keepdims=True)
        var = jnp.mean(jnp.square(o - mu), axis=-1, keepdims=True)
        on = (o - mu) * lax.rsqrt(var + EPS)
        gate = gate_ref[0, _chunk_rows(c), _head_cols(h)].astype(F32)
        o_ref[0, _chunk_rows(c), _head_cols(h)] = (gate * on).astype(BF16)


def _ret_out(rq, rk, rkf, rv, gate, sb, ld_f, ld_b):
    b, s, _ = rq.shape
    ts = min(SEQ_TILE, s)
    nt = s // ts
    cpt = ts // RET_CHUNK
    main = lambda i, t: (i, t, 0)
    seq_spec = pl.BlockSpec((1, ts, RET_WIDTH), main)
    return pl.pallas_call(
        _ret_out_kernel,
        grid=(b, nt),
        in_specs=[pl.BlockSpec(memory_space=pltpu.SMEM), pl.BlockSpec(memory_space=pltpu.SMEM),
                  seq_spec, seq_spec, seq_spec, seq_spec, seq_spec,
                  pl.BlockSpec((1, cpt, RET_HEAD_DIM, RET_WIDTH), lambda i, t: (i, t, 0, 0))],
        out_specs=seq_spec,
        out_shape=jax.ShapeDtypeStruct((b, s, RET_WIDTH), BF16),
        scratch_shapes=[pltpu.VMEM((RET_HEAD_DIM, RET_WIDTH), F32),
                        pltpu.VMEM((3, N_RET_HEADS, RET_CHUNK, RET_HEAD_DIM), F32),
                        pltpu.VMEM((cpt * N_RET_HEADS, RET_CHUNK, RET_CHUNK), BF16),
                        pltpu.VMEM((cpt, RET_HEAD_DIM, RET_WIDTH), F32),
                        pltpu.VMEM((cpt, RET_HEAD_DIM, 2 * RET_WIDTH), BF16)],
        compiler_params=pltpu.CompilerParams(dimension_semantics=("parallel", "arbitrary"),
                                             vmem_limit_bytes=VMEM_LIMIT),
        name="ret_out",
    )(ld_f, ld_b, rq, rk, rkf, rv, gate, sb)


def _out_ffn_kernel(h_ref, a_ref, r_ref, woa_ref, wor_ref, g2_ref, wg_ref, wu_ref, wd_ref, gf_ref,
                    y_ref, act_ref):
    groups = _row_groups(h_ref.shape[0])
    hs = [h_ref[r, :] + _dot(a_ref[r, :], woa_ref[...]) + _dot(r_ref[r, :], wor_ref[...]) for r in groups]
    hns = [_rmsnorm(h, g2_ref[...]).astype(BF16) for h in hs]
    for r, hn in zip(groups, hns):
        _gate_up(hn, r, wg_ref, wu_ref, act_ref)
    ys = [_dot(act_ref[r, :], wd_ref[...]) for r in groups]
    for r, h, y in zip(groups, hs, ys):
        y_ref[r, :] = _rmsnorm(h + 0.5 * y, gf_ref[...])


def _out_ffn(h, a, r, woa, wor, g2, wg, wu, wd, gf):
    n = h.shape[0]
    tm = min(OUT_TOKEN_TILE, n)
    row = lambda i: (i, 0)
    return pl.pallas_call(
        _out_ffn_kernel,
        grid=(n // tm,),
        in_specs=[
            pl.BlockSpec((tm, D_MODEL), row),
            pl.BlockSpec((tm, ATTN_WIDTH), row),
            pl.BlockSpec((tm, RET_WIDTH), row),
            _const_spec((ATTN_WIDTH, D_MODEL)),
            _const_spec((RET_WIDTH, D_MODEL)),
            _const_spec((1, D_MODEL)),
            _const_spec((D_MODEL, D_FF)),
            _const_spec((D_MODEL, D_FF)),
            _const_spec((D_FF, D_MODEL)),
            _const_spec((1, D_MODEL)),
        ],
        out_specs=pl.BlockSpec((tm, D_MODEL), row),
        out_shape=jax.ShapeDtypeStruct((n, D_MODEL), F32),
        scratch_shapes=[pltpu.VMEM((tm, D_FF), BF16)],
        compiler_params=pltpu.CompilerParams(dimension_semantics=("parallel",),
                                             vmem_limit_bytes=VMEM_LIMIT),
        name="out_ffn",
    )(h, a, r, woa, wor, g2, wg, wu, wd, gf)


def _rope_tables(seq_len):
    half = ROT_DIM // 2
    lane = jnp.arange(LANES) % ATTN_HEAD_DIM
    inv_a = ROPE_THETA ** (-jnp.arange(0, ROT_DIM, 2, dtype=F32) / ROT_DIM)
    freq_a = jnp.where(lane < ROT_DIM, inv_a[lane % half], 0.0)
    inv_r = RET_ROT_THETA ** (-jnp.linspace(0.0, 1.0, RET_HEAD_DIM // 2, dtype=F32))
    freq_r = jnp.concatenate([inv_r, inv_r])

    start = (jnp.arange(seq_len // ROPE_BLOCK, dtype=F32) * ROPE_BLOCK)[:, None, None]
    offset = jnp.arange(ROPE_BLOCK, dtype=F32)[None, :, None]

    def cos_sin(freq):
        a, b = start * freq, offset * freq
        cos = jnp.cos(a) * jnp.cos(b) - jnp.sin(a) * jnp.sin(b)
        sin = jnp.sin(a) * jnp.cos(b) + jnp.cos(a) * jnp.sin(b)
        return cos.reshape(seq_len, LANES), sin.reshape(seq_len, LANES)

    cos_a, sin = cos_sin(freq_a)
    sin_a = sin * jnp.where(lane < half, -1.0, 0.0)[None, :]
    sin_b = sin * jnp.where((lane >= half) & (lane < ROT_DIM), 1.0, 0.0)[None, :]
    cos_r, sin = cos_sin(freq_r)
    sign_r = jnp.where(jnp.arange(LANES) < RET_HEAD_DIM // 2, -1.0, 1.0)
    return jnp.concatenate([cos_a, sin_a, sin_b, cos_r, sin * sign_r[None, :]], axis=1)


def _layer(x, p, tables):
    b, s, _ = x.shape
    n = b * s
    h, q, k, v, rq, rk, rkf, rkb, rv, gate = _ffn_proj(
        x.reshape(n, D_MODEL), s, p["g1"], p["wg1"], p["wu1"], p["wd1"], p["gm"],
        p["win"], tables,
        p["ldf"], p["ldb"])
    seq = lambda a: a.reshape(b, s, a.shape[-1])
    attn = _attn(seq(q), seq(k), seq(v), p["sink"], p["ga"])
    sb = _ret_state(seq(rkb), seq(rv), p["ldb"])
    ret = _ret_out(seq(rq), seq(rk), seq(rkf), seq(rv), seq(gate), sb, p["ldf"], p["ldb"])
    y = _out_ffn(h, attn.reshape(n, ATTN_WIDTH), ret.reshape(n, RET_WIDTH),
                 p["woa"], p["wor"], p["g2"], p["wg2"], p["wu2"], p["wd2"], p["gf"])
    return y.reshape(b, s, D_MODEL)


def kernel(x_prompt, x_sample, ffn1_norm, ffn1_w_gate, ffn1_w_up, ffn1_w_down, mix_norm, w_in, attn_sink, attn_out_norm, ret_log_decay_fwd, ret_log_decay_bwd, w_out, ffn2_norm, ffn2_w_gate, ffn2_w_up, ffn2_w_down, final_norm):
    assert ffn1_norm.shape[0] == 1
    wo = w_out[0].astype(BF16)
    p = dict(
        g1=ffn1_norm[0][None, :], wg1=ffn1_w_gate[0].astype(BF16), wu1=ffn1_w_up[0].astype(BF16),
        wd1=ffn1_w_down[0].astype(BF16), gm=mix_norm[0][None, :],
        win=w_in[0].astype(BF16),
        sink=attn_sink[0], ga=attn_out_norm[0][None, :],
        ldf=ret_log_decay_fwd[0], ldb=ret_log_decay_bwd[0],
        woa=wo[:ATTN_WIDTH], wor=wo[ATTN_WIDTH:],
        g2=ffn2_norm[0][None, :], wg2=ffn2_w_gate[0].astype(BF16), wu2=ffn2_w_up[0].astype(BF16),
        wd2=ffn2_w_down[0].astype(BF16), gf=final_norm[None, :],
    )
    tables = _rope_tables(max(x_prompt.shape[1], x_sample.shape[1]))
    return (_layer(x_prompt, p, tables), _layer(x_sample, p, tables))
```

```python
import functools

import jax
import jax.numpy as jnp
from jax import lax
from jax.experimental import pallas as pl
from jax.experimental.pallas import tpu as pltpu

D_MODEL = 1024
D_FF = 2816
N_ATTN_HEADS = 8
N_KV_HEADS = 2
ATTN_HEAD_DIM = 64
ATTN_WIDTH = N_ATTN_HEADS * ATTN_HEAD_DIM
WINDOW = 128
ATTN_BLOCK = 128
ROT_DIM = 16
ROPE_THETA = 500000.0
RET_HEAD_DIM = 128
N_RET_HEADS = 4
RET_WIDTH = N_RET_HEADS * RET_HEAD_DIM
RET_CHUNK = 128
RET_ROT_THETA = 10000.0
EPS = 1e-6
NEG_BIG = -1e30
LOG2_E = 1.4426950408889634
ATTN_Q_SCALE = ATTN_HEAD_DIM ** -0.5 * LOG2_E

LANES = 128
KV_DUP_WIDTH = 2 * LANES
FF_CHUNK = 256
PROJ_CHUNK = 512
ROW_GROUP = 256
TOKEN_TILE = 512
OUT_TOKEN_TILE = 1024
SEQ_TILE = 1024
RET_STATE_TILE = 2048
ROPE_BLOCK = 128
VMEM_LIMIT = 56 * 1024 * 1024

BF16 = jnp.bfloat16
F32 = jnp.float32


def _dot(a, b):
    return jnp.dot(a, b, preferred_element_type=F32)


def _dot_nt(a, b):
    return lax.dot_general(a, b, (((1,), (1,)), ((), ())), preferred_element_type=F32)


def _dot_tn(a, b):
    return lax.dot_general(a, b, (((0,), (0,)), ((), ())), preferred_element_type=F32)


def _rmsnorm(x, g):
    return x * lax.rsqrt(jnp.mean(x * x, axis=-1, keepdims=True) + EPS) * g


def _silu(x):
    return x * (1.0 / (1.0 + jnp.exp(-x)))


def _row_groups(rows):
    return [slice(i, i + ROW_GROUP) for i in range(0, rows, ROW_GROUP)]


def _gate_up(xn, rows, wg_ref, wu_ref, act_ref):
    for c in range(D_FF // FF_CHUNK):
        cols = slice(c * FF_CHUNK, (c + 1) * FF_CHUNK)
        g = _dot(xn, wg_ref[:, cols])
        u = _dot(xn, wu_ref[:, cols])
        act_ref[rows, cols] = (_silu(g) * u).astype(BF16)


def _ffn_proj_kernel(ldf_ref, ldb_ref, x_ref, g1_ref, wg_ref, wu_ref, wd_ref, gm_ref,
                     win_ref,
                     rope_ref,
                     h_ref, q_ref, k_ref, v_ref, rq_ref, rk_ref, rkf_ref, rkb_ref, rv_ref, gate_ref,
                     act_ref, decay_ref):
    @pl.when(pl.program_id(0) == 0)
    def _():
        shape = decay_ref.shape[1:]
        j = (lax.broadcasted_iota(jnp.int32, shape, 0) % RET_CHUNK).astype(F32)
        decay_ref[0] = jnp.exp(_head_lane_values(ldf_ref) * (float(RET_CHUNK) - 1.0 - j))
        decay_ref[1] = jnp.exp(_head_lane_values(ldb_ref) * j)

    groups = _row_groups(x_ref.shape[0])
    xns = [_rmsnorm(x_ref[r, :], g1_ref[...]).astype(BF16) for r in groups]
    for r, xn in zip(groups, xns):
        _gate_up(xn, r, wg_ref, wu_ref, act_ref)
    ys = [_dot(act_ref[r, :], wd_ref[...]) for r in groups]

    def lane_blocks(p):
        return [p[:, c * LANES:(c + 1) * LANES] for c in range(p.shape[1] // LANES)]

    for r, y in zip(groups, ys):
        h = x_ref[r, :] + 0.5 * y
        h_ref[r, :] = h
        un = _rmsnorm(h, gm_ref[...]).astype(BF16)

        cosa, sina, sinb, cosr, sinr = (rope_ref[r, t * LANES:(t + 1) * LANES] for t in range(5))

        def rope_a(xb):
            return (xb * cosa + pltpu.roll(xb, LANES - ROT_DIM // 2, 1) * sina
                    + pltpu.roll(xb, ROT_DIM // 2, 1) * sinb)

        def rope_r(xb):
            return xb * cosr + pltpu.roll(xb, RET_HEAD_DIM // 2, 1) * sinr

        def proj(section):
            kv_cols = 2 * N_KV_HEADS * ATTN_HEAD_DIM
            if section == 0:
                cols = slice(0, ATTN_WIDTH)
            elif section == 1:
                cols = slice(ATTN_WIDTH, ATTN_WIDTH + kv_cols)
            else:
                start = ATTN_WIDTH + kv_cols + (section - 2) * PROJ_CHUNK
                cols = slice(start, start + PROJ_CHUNK)
            return _dot(un, win_ref[:, cols])

        for c, blk in enumerate(lane_blocks(proj(3))):
            cols = slice(c * LANES, (c + 1) * LANES)
            rk = rope_r(blk) * (RET_HEAD_DIM ** -0.5)
            rk_ref[r, cols] = rk.astype(BF16)
            rkf_ref[r, cols] = (rk * decay_ref[0, r, cols]).astype(BF16)
            rkb_ref[r, cols] = (rk * decay_ref[1, r, cols]).astype(BF16)
        gate_ref[r, :] = _silu(proj(5)).astype(BF16)
        for c, blk in enumerate(lane_blocks(proj(0))):
            q_ref[r, c * LANES:(c + 1) * LANES] = (rope_a(blk) * ATTN_Q_SCALE).astype(BF16)
        for c, blk in enumerate(lane_blocks(proj(2))):
            rq_ref[r, c * LANES:(c + 1) * LANES] = rope_r(blk).astype(BF16)
        kblk, vblk = lane_blocks(proj(1))
        lo_lanes = lax.broadcasted_iota(jnp.int32, kblk.shape, 1) < ATTN_HEAD_DIM
        for ref, blk in ((k_ref, rope_a(kblk)), (v_ref, vblk)):
            swapped = pltpu.roll(blk, ATTN_HEAD_DIM, 1)
            ref[r, 0:LANES] = jnp.where(lo_lanes, blk, swapped).astype(BF16)
            ref[r, LANES:2 * LANES] = jnp.where(lo_lanes, swapped, blk).astype(BF16)
        rv_ref[r, :] = proj(4).astype(BF16)


def _const_spec(shape):
    return pl.BlockSpec(shape, lambda *_: (0,) * len(shape), pipeline_mode=pl.Buffered(1))


def _ffn_proj(x2d, seq_len, g1, wg, wu, wd, gm, win, tables, ld_f, ld_b):
    n = x2d.shape[0]
    tm = min(TOKEN_TILE, seq_len)
    assert tm % RET_CHUNK == 0
    tiles_per_seq = seq_len // tm
    row = lambda i: (i, 0)
    tab = lambda i: (i % tiles_per_seq, 0)
    out_widths = (ATTN_WIDTH, KV_DUP_WIDTH, KV_DUP_WIDTH) + (RET_WIDTH,) * 6
    return pl.pallas_call(
        _ffn_proj_kernel,
        grid=(n // tm,),
        in_specs=[
            pl.BlockSpec(memory_space=pltpu.SMEM),
            pl.BlockSpec(memory_space=pltpu.SMEM),
            pl.BlockSpec((tm, D_MODEL), row),
            _const_spec((1, D_MODEL)),
            _const_spec((D_MODEL, D_FF)),
            _const_spec((D_MODEL, D_FF)),
            _const_spec((D_FF, D_MODEL)),
            _const_spec((1, D_MODEL)),
            _const_spec(win.shape),
            pl.BlockSpec((tm, tables.shape[1]), tab),
        ],
        out_specs=[pl.BlockSpec((tm, D_MODEL), row)] + [pl.BlockSpec((tm, w), row) for w in out_widths],
        out_shape=[jax.ShapeDtypeStruct((n, D_MODEL), F32)]
                  + [jax.ShapeDtypeStruct((n, w), BF16) for w in out_widths],
        scratch_shapes=[pltpu.VMEM((tm, D_FF), BF16),
                        pltpu.VMEM((2, tm, RET_WIDTH), F32)],
        compiler_params=pltpu.CompilerParams(dimension_semantics=("arbitrary",),
                                             vmem_limit_bytes=VMEM_LIMIT),
        name="ffn_proj",
    )(ld_f, ld_b, x2d, g1, wg, wu, wd, gm, win, tables)


def _attn_kernel(sink_ref, q_ref, kp_ref, km_ref, kn_ref, vp_ref, vm_ref, vn_ref, gain_ref,
                 o_ref, klo_ref, khi_ref, vlo_ref, vhi_ref, bias_ref, s_ref, p_ref, e_ref, acc_ref):
    t = pl.program_id(1)
    nt = pl.num_programs(1)
    ts = q_ref.shape[1]
    nblk = ts // ATTN_BLOCK
    wk = 3 * ATTN_BLOCK

    @pl.when(t == 0)
    def _():
        qi = lax.broadcasted_iota(jnp.int32, (ATTN_BLOCK, wk), 0)
        kj = lax.broadcasted_iota(jnp.int32, (ATTN_BLOCK, wk), 1)
        band = (kj >= qi) & (kj <= qi + 2 * WINDOW)
        bias_ref[0] = jnp.where(band, 0.0, NEG_BIG)
        bias_ref[1] = jnp.where(band & (kj >= ATTN_BLOCK), 0.0, NEG_BIG)
        bias_ref[2] = jnp.where(band & (kj < 2 * ATTN_BLOCK), 0.0, NEG_BIG)

    lo_lanes = (lax.broadcasted_iota(jnp.int32, (1, KV_DUP_WIDTH), 1) % LANES) < ATTN_HEAD_DIM
    window = ((slice(0, ATTN_BLOCK), kp_ref, vp_ref), (slice(ATTN_BLOCK, ATTN_BLOCK + ts), km_ref, vm_ref),
              (slice(ATTN_BLOCK + ts, 2 * ATTN_BLOCK + ts), kn_ref, vn_ref))
    for dst, k_src, v_src in window:
        kblk, vblk = k_src[0], v_src[0]
        klo_ref[dst, :] = jnp.where(lo_lanes, kblk, jnp.zeros_like(kblk))
        khi_ref[dst, :] = jnp.where(lo_lanes, jnp.zeros_like(kblk), kblk)
        vlo_ref[dst, :] = jnp.where(lo_lanes, vblk, jnp.ones_like(vblk))
        vhi_ref[dst, :] = jnp.where(lo_lanes, jnp.ones_like(vblk), vblk)

    out_lo = lax.broadcasted_iota(jnp.int32, (ATTN_BLOCK, LANES), 1) < ATTN_HEAD_DIM
    first_idx = jnp.where(t == 0, 1, 0)
    last_idx = jnp.where(t == nt - 1, 2, 0)

    units = [(j, g, pair, half) for j in range(nblk) for g in range(N_KV_HEADS)
             for pair in range(2) for half in range(2)]

    def rows_of(j):
        return slice(j * ATTN_BLOCK, (j + 1) * ATTN_BLOCK)

    def win_rows(j):
        return slice(j * ATTN_BLOCK, j * ATTN_BLOCK + wk)

    def pair_cols(g, pair):
        return slice((2 * g + pair) * LANES, (2 * g + pair + 1) * LANES)

    for u, (j, g, pair, half) in enumerate(units):
        k_ref = khi_ref if half else klo_ref
        bias_idx = first_idx if j == 0 else (last_idx if j == nblk - 1 else 0)
        s = _dot_nt(q_ref[0, rows_of(j), pair_cols(g, pair)],
                    k_ref[win_rows(j), g * LANES:(g + 1) * LANES])
        for blk in range(3):
            cols = slice(blk * ATTN_BLOCK, (blk + 1) * ATTN_BLOCK)
            s_ref[u, :, cols] = s[:, cols] if blk == 1 else s[:, cols] + bias_ref[bias_idx, :, cols]

    for u, (j, g, pair, half) in enumerate(units):
        s = s_ref[u]
        sink = sink_ref[4 * g + 2 * pair + half] * LOG2_E
        m = jnp.maximum(jnp.max(s, axis=-1, keepdims=True), sink)
        p_ref[u] = jnp.exp2(s - m).astype(BF16)
        e_ref[u] = jnp.broadcast_to(jnp.exp2(sink - m), (ATTN_BLOCK, LANES))

    for u in range(0, len(units), 2):
        j, g, pair, _ = units[u]
        kv_cols = slice(g * LANES, (g + 1) * LANES)
        even = _dot(p_ref[u], vlo_ref[win_rows(j), kv_cols])
        odd = _dot(p_ref[u + 1], vhi_ref[win_rows(j), kv_cols])
        numer = jnp.where(out_lo, even, odd)
        rowsum = pltpu.roll(jnp.where(out_lo, odd, even), ATTN_HEAD_DIM, 1)
        denom = rowsum + jnp.where(out_lo, e_ref[u], e_ref[u + 1])
        acc_ref[rows_of(j), pair_cols(g, pair)] = numer * (1.0 / denom)

    o_ref[0] = _rmsnorm(acc_ref[...], gain_ref[...]).astype(BF16)


def _attn_specs(b, s, ts):
    bps = ts // ATTN_BLOCK
    nb = s // ATTN_BLOCK
    assert bps >= 2
    main = lambda i, t: (i, t, 0)
    prev = lambda i, t: (i, jnp.maximum(t * bps - 1, 0), 0)
    nxt = lambda i, t: (i, jnp.minimum((t + 1) * bps, nb - 1), 0)
    kv_specs = [pl.BlockSpec((1, ATTN_BLOCK, KV_DUP_WIDTH), prev),
                pl.BlockSpec((1, ts, KV_DUP_WIDTH), main),
                pl.BlockSpec((1, ATTN_BLOCK, KV_DUP_WIDTH), nxt)]
    in_specs = ([pl.BlockSpec(memory_space=pltpu.SMEM), pl.BlockSpec((1, ts, ATTN_WIDTH), main)]
                + kv_specs + kv_specs + [pl.BlockSpec((1, ATTN_WIDTH), lambda i, t: (0, 0))])
    scratch = [pltpu.VMEM((ts + 2 * ATTN_BLOCK, KV_DUP_WIDTH), BF16)] * 4 + [
        pltpu.VMEM((3, ATTN_BLOCK, 3 * ATTN_BLOCK), F32),
        pltpu.VMEM((bps * N_ATTN_HEADS, ATTN_BLOCK, 3 * ATTN_BLOCK), F32),
        pltpu.VMEM((bps * N_ATTN_HEADS, ATTN_BLOCK, 3 * ATTN_BLOCK), BF16),
        pltpu.VMEM((bps * N_ATTN_HEADS, ATTN_BLOCK, LANES), F32),
        pltpu.VMEM((ts, ATTN_WIDTH), F32)]
    return (in_specs, pl.BlockSpec((1, ts, ATTN_WIDTH), main),
            jax.ShapeDtypeStruct((b, s, ATTN_WIDTH), BF16), scratch)


def _head_lane_values(ld_ref):
    lane = lax.broadcasted_iota(jnp.int32, (1, RET_WIDTH), 1)
    ld = jnp.zeros((1, RET_WIDTH), F32)
    for h in range(N_RET_HEADS):
        ld = jnp.where(lane >= h * RET_HEAD_DIM, ld_ref[h], ld)
    return ld


def _chunk_decay(ld_ref):
    return jnp.exp(_head_lane_values(ld_ref) * float(RET_CHUNK))


def _head_cols(h):
    return slice(h * RET_HEAD_DIM, (h + 1) * RET_HEAD_DIM)


def _chunk_rows(c):
    return slice(c * RET_CHUNK, (c + 1) * RET_CHUNK)


def _ret_state_kernel(ldb_ref, rkb_ref, rv_ref, sb_ref, state_ref, kv_ref):
    t = pl.program_id(1)
    nchunk = rkb_ref.shape[1] // RET_CHUNK

    @pl.when(t == 0)
    def _():
        state_ref[...] = jnp.zeros_like(state_ref)

    for c in range(nchunk):
        for h in range(N_RET_HEADS):
            kv_ref[c, :, _head_cols(h)] = _dot_tn(rkb_ref[0, _chunk_rows(c), _head_cols(h)],
                                                  rv_ref[0, _chunk_rows(c), _head_cols(h)])

    chunk_decay = _chunk_decay(ldb_ref)
    state = state_ref[...]
    for c in reversed(range(nchunk)):
        sb_ref[0, c] = state.astype(BF16)
        state = state * chunk_decay + kv_ref[c]
    state_ref[...] = state


def _ret_state(rkb, rv, ld_b):
    b, s, _ = rkb.shape
    ts = min(RET_STATE_TILE, s)
    nt = s // ts
    cpt = ts // RET_CHUNK
    rev = lambda i, t: (i, nt - 1 - t, 0)
    return pl.pallas_call(
        _ret_state_kernel,
        grid=(b, nt),
        in_specs=[pl.BlockSpec(memory_space=pltpu.SMEM),
                  pl.BlockSpec((1, ts, RET_WIDTH), rev),
                  pl.BlockSpec((1, ts, RET_WIDTH), rev)],
        out_specs=pl.BlockSpec((1, cpt, RET_HEAD_DIM, RET_WIDTH), lambda i, t: (i, nt - 1 - t, 0, 0)),
        out_shape=jax.ShapeDtypeStruct((b, s // RET_CHUNK, RET_HEAD_DIM, RET_WIDTH), BF16),
        scratch_shapes=[pltpu.VMEM((RET_HEAD_DIM, RET_WIDTH), F32),
                        pltpu.VMEM((cpt, RET_HEAD_DIM, RET_WIDTH), F32)],
        compiler_params=pltpu.CompilerParams(dimension_semantics=("parallel", "arbitrary"),
                                             vmem_limit_bytes=VMEM_LIMIT),
        name="ret_state",
    )(ld_b, rkb, rv)


def _ret_out_kernel(ldf_ref, ldb_ref, rq_ref, rk_ref, rkf_ref, rv_ref, gate_ref, sb_ref, o_ref,
                    state_ref, tab_ref, inner_ref, kv_ref, cs_ref):
    t = pl.program_id(1)
    nchunk = rq_ref.shape[1] // RET_CHUNK
    MASK, Q_FWD, Q_BWD = range(3)

    @pl.when(t == 0)
    def _():
        state_ref[...] = jnp.zeros_like(state_ref)
        shape = (RET_CHUNK, RET_HEAD_DIM)
        row_i = lax.broadcasted_iota(jnp.int32, shape, 0)
        col_i = lax.broadcasted_iota(jnp.int32, shape, 1)
        row = row_i.astype(F32)
        diff = (row_i - col_i).astype(F32)
        for h in range(N_RET_HEADS):
            d_fwd = jnp.where(diff >= 0, jnp.exp(ldf_ref[h] * jnp.maximum(diff, 0.0)), 0.0)
            d_bwd = jnp.where(diff <= 0, jnp.exp(ldb_ref[h] * jnp.maximum(-diff, 0.0)), 0.0)
            tab_ref[MASK, h] = d_fwd + d_bwd
            tab_ref[Q_FWD, h] = jnp.exp(ldf_ref[h] * (row + 1.0))
            tab_ref[Q_BWD, h] = jnp.exp(ldb_ref[h] * (float(RET_CHUNK) - row))

    units = [(c, h) for c in range(nchunk) for h in range(N_RET_HEADS)]

    for u, (c, h) in enumerate(units):
        q = rq_ref[0, _chunk_rows(c), _head_cols(h)]
        k = rk_ref[0, _chunk_rows(c), _head_cols(h)]
        inner_ref[u] = (_dot_nt(q, k) * tab_ref[MASK, h]).astype(BF16)
        kv_ref[c, :, _head_cols(h)] = _dot_tn(rkf_ref[0, _chunk_rows(c), _head_cols(h)],
                                              rv_ref[0, _chunk_rows(c), _head_cols(h)])

    chunk_decay = _chunk_decay(ldf_ref)
    state = state_ref[...]
    for c in range(nchunk):
        fwd = state.astype(BF16)
        for h in range(N_RET_HEADS):
            base = 2 * h * RET_HEAD_DIM
            cs_ref[c, :, base:base + RET_HEAD_DIM] = fwd[:, _head_cols(h)]
            cs_ref[c, :, base + RET_HEAD_DIM:base + 2 * RET_HEAD_DIM] = sb_ref[0, c, :, _head_cols(h)]
        state = state * chunk_decay + kv_ref[c]
    state_ref[...] = state

    for u, (c, h) in enumerate(units):
        q = rq_ref[0, _chunk_rows(c), _head_cols(h)]
        o = _dot(inner_ref[u], rv_ref[0, _chunk_rows(c), _head_cols(h)])
        cross = _dot(q, cs_ref[c, :, 2 * h * RET_HEAD_DIM:2 * (h + 1) * RET_HEAD_DIM])
        o += tab_ref[Q_FWD, h] * cross[:, :RET_HEAD_DIM] + tab_ref[Q_BWD, h] * cross[:, RET_HEAD_DIM:]
        mu = jnp.mean(o, axis=-1, keepdims=True)
        var = jnp.mean(jnp.square(o - mu), axis=-1, keepdims=True)
        on = (o - mu) * lax.rsqrt(var + EPS)
        gate = gate_ref[0, _chunk_rows(c), _head_cols(h)].astype(F32)
        o_ref[0, _chunk_rows(c), _head_cols(h)] = (gate * on).astype(BF16)


def _ret_out_specs(b, s, ts):
    cpt = ts // RET_CHUNK
    seq_spec = pl.BlockSpec((1, ts, RET_WIDTH), lambda i, t: (i, t, 0))
    in_specs = [pl.BlockSpec(memory_space=pltpu.SMEM), pl.BlockSpec(memory_space=pltpu.SMEM),
                seq_spec, seq_spec, seq_spec, seq_spec, seq_spec,
                pl.BlockSpec((1, cpt, RET_HEAD_DIM, RET_WIDTH), lambda i, t: (i, t, 0, 0))]
    scratch = [pltpu.VMEM((RET_HEAD_DIM, RET_WIDTH), F32),
               pltpu.VMEM((3, N_RET_HEADS, RET_CHUNK, RET_HEAD_DIM), F32),
               pltpu.VMEM((cpt * N_RET_HEADS, RET_CHUNK, RET_CHUNK), BF16),
               pltpu.VMEM((cpt, RET_HEAD_DIM, RET_WIDTH), F32),
               pltpu.VMEM((cpt, RET_HEAD_DIM, 2 * RET_WIDTH), BF16)]
    return in_specs, seq_spec, jax.ShapeDtypeStruct((b, s, RET_WIDTH), BF16), scratch


def _mix(attn_args, ret_args):
    sink, q, k, v, gain = attn_args
    b, s, _ = q.shape
    ts = min(SEQ_TILE, s)
    a_in, a_out, a_shape, a_scratch = _attn_specs(b, s, ts)
    r_in, r_out, r_shape, r_scratch = _ret_out_specs(b, s, ts)
    n_a_in, n_r_in, n_a_scr = len(a_in), len(r_in), len(a_scratch)

    def mix_kernel(*refs):
        ins, (attn_o, ret_o), scr = refs[:n_a_in + n_r_in], refs[n_a_in + n_r_in:][:2], refs[n_a_in + n_r_in + 2:]
        _ret_out_kernel(*ins[n_a_in:], ret_o, *scr[n_a_scr:])
        _attn_kernel(*ins[:n_a_in], attn_o, *scr[:n_a_scr])

    return pl.pallas_call(
        mix_kernel,
        grid=(b, s // ts),
        in_specs=a_in + r_in,
        out_specs=[a_out, r_out],
        out_shape=[a_shape, r_shape],
        scratch_shapes=a_scratch + r_scratch,
        compiler_params=pltpu.CompilerParams(dimension_semantics=("parallel", "arbitrary"),
                                             vmem_limit_bytes=VMEM_LIMIT),
        name="mix",
    )(sink, q, k, k, k, v, v, v, gain, *ret_args)


def _out_ffn_kernel(h_ref, a_ref, r_ref, woa_ref, wor_ref, g2_ref, wg_ref, wu_ref, wd_ref, gf_ref,
                    y_ref, act_ref):
    groups = _row_groups(h_ref.shape[0])
    hs = [h_ref[r, :] + _dot(a_ref[r, :], woa_ref[...]) + _dot(r_ref[r, :], wor_ref[...]) for r in groups]
    hns = [_rmsnorm(h, g2_ref[...]).astype(BF16) for h in hs]
    for r, hn in zip(groups, hns):
        _gate_up(hn, r, wg_ref, wu_ref, act_ref)
    ys = [_dot(act_ref[r, :], wd_ref[...]) for r in groups]
    for r, h, y in zip(groups, hs, ys):
        y_ref[r, :] = _rmsnorm(h + 0.5 * y, gf_ref[...])


def _out_ffn(h, a, r, woa, wor, g2, wg, wu, wd, gf):
    n = h.shape[0]
    tm = min(OUT_TOKEN_TILE, n)
    row = lambda i: (i, 0)
    return pl.pallas_call(
        _out_ffn_kernel,
        grid=(n // tm,),
        in_specs=[
            pl.BlockSpec((tm, D_MODEL), row),
            pl.BlockSpec((tm, ATTN_WIDTH), row),
            pl.BlockSpec((tm, RET_WIDTH), row),
            _const_spec((ATTN_WIDTH, D_MODEL)),
            _const_spec((RET_WIDTH, D_MODEL)),
            _const_spec((1, D_MODEL)),
            _const_spec((D_MODEL, D_FF)),
            _const_spec((D_MODEL, D_FF)),
            _const_spec((D_FF, D_MODEL)),
            _const_spec((1, D_MODEL)),
        ],
        out_specs=pl.BlockSpec((tm, D_MODEL), row),
        out_shape=jax.ShapeDtypeStruct((n, D_MODEL), F32),
        scratch_shapes=[pltpu.VMEM((tm, D_FF), BF16)],
        compiler_params=pltpu.CompilerParams(dimension_semantics=("parallel",),
                                             vmem_limit_bytes=VMEM_LIMIT),
        name="out_ffn",
    )(h, a, r, woa, wor, g2, wg, wu, wd, gf)


def _rope_tables(seq_len):
    half = ROT_DIM // 2
    lane = jnp.arange(LANES) % ATTN_HEAD_DIM
    inv_a = ROPE_THETA ** (-jnp.arange(0, ROT_DIM, 2, dtype=F32) / ROT_DIM)
    freq_a = jnp.where(lane < ROT_DIM, inv_a[lane % half], 0.0)
    inv_r = RET_ROT_THETA ** (-jnp.linspace(0.0, 1.0, RET_HEAD_DIM // 2, dtype=F32))
    freq_r = jnp.concatenate([inv_r, inv_r])

    start = (jnp.arange(seq_len // ROPE_BLOCK, dtype=F32) * ROPE_BLOCK)[:, None, None]
    offset = jnp.arange(ROPE_BLOCK, dtype=F32)[None, :, None]

    def cos_sin(freq):
        a, b = start * freq, offset * freq
        cos = jnp.cos(a) * jnp.cos(b) - jnp.sin(a) * jnp.sin(b)
        sin = jnp.sin(a) * jnp.cos(b) + jnp.cos(a) * jnp.sin(b)
        return cos.reshape(seq_len, LANES), sin.reshape(seq_len, LANES)

    cos_a, sin = cos_sin(freq_a)
    sin_a = sin * jnp.where(lane < half, -1.0, 0.0)[None, :]
    sin_b = sin * jnp.where((lane >= half) & (lane < ROT_DIM), 1.0, 0.0)[None, :]
    cos_r, sin = cos_sin(freq_r)
    sign_r = jnp.where(jnp.arange(LANES) < RET_HEAD_DIM // 2, -1.0, 1.0)
    return jnp.concatenate([cos_a, sin_a, sin_b, cos_r, sin * sign_r[None, :]], axis=1)


def _layer(x, p, tables):
    b, s, _ = x.shape
    n = b * s
    h, q, k, v, rq, rk, rkf, rkb, rv, gate = _ffn_proj(
        x.reshape(n, D_MODEL), s, p["g1"], p["wg1"], p["wu1"], p["wd1"], p["gm"],
        p["win"], tables,
        p["ldf"], p["ldb"])
    seq = lambda a: a.reshape(b, s, a.shape[-1])
    sb = _ret_state(seq(rkb), seq(rv), p["ldb"])
    attn, ret = _mix((p["sink"], seq(q), seq(k), seq(v), p["ga"]),
                     (p["ldf"], p["ldb"], seq(rq), seq(rk), seq(rkf), seq(rv), seq(gate), sb))
    y = _out_ffn(h, attn.reshape(n, ATTN_WIDTH), ret.reshape(n, RET_WIDTH),
                 p["woa"], p["wor"], p["g2"], p["wg2"], p["wu2"], p["wd2"], p["gf"])
    return y.reshape(b, s, D_MODEL)


def kernel(x_prompt, x_sample, ffn1_norm, ffn1_w_gate, ffn1_w_up, ffn1_w_down, mix_norm, w_in, attn_sink, attn_out_norm, ret_log_decay_fwd, ret_log_decay_bwd, w_out, ffn2_norm, ffn2_w_gate, ffn2_w_up, ffn2_w_down, final_norm):
    assert ffn1_norm.shape[0] == 1
    wo = w_out[0].astype(BF16)
    p = dict(
        g1=ffn1_norm[0][None, :], wg1=ffn1_w_gate[0].astype(BF16), wu1=ffn1_w_up[0].astype(BF16),
        wd1=ffn1_w_down[0].astype(BF16), gm=mix_norm[0][None, :],
        win=w_in[0].astype(BF16),
        sink=attn_sink[0], ga=attn_out_norm[0][None, :],
        ldf=ret_log_decay_fwd[0], ldb=ret_log_decay_bwd[0],
        woa=wo[:ATTN_WIDTH], wor=wo[ATTN_WIDTH:],
        g2=ffn2_norm[0][None, :], wg2=ffn2_w_gate[0].astype(BF16), wu2=ffn2_w_up[0].astype(BF16),
        wd2=ffn2_w_down[0].astype(BF16), gf=final_norm[None, :],
    )
    tables = _rope_tables(max(x_prompt.shape[1], x_sample.shape[1]))
    return (_layer(x_prompt, p, tables), _layer(x_sample, p, tables))
```

```python
import functools

import jax
import jax.numpy as jnp
from jax import lax
from jax.experimental import pallas as pl
from jax.experimental.pallas import tpu as pltpu

D_MODEL = 1024
D_FF = 2816
N_ATTN_HEADS = 8
N_KV_HEADS = 2
ATTN_HEAD_DIM = 64
ATTN_WIDTH = N_ATTN_HEADS * ATTN_HEAD_DIM
WINDOW = 128
ATTN_BLOCK = 128
ROT_DIM = 16
ROPE_THETA = 500000.0
RET_HEAD_DIM = 128
N_RET_HEADS = 4
RET_WIDTH = N_RET_HEADS * RET_HEAD_DIM
RET_CHUNK = 128
RET_ROT_THETA = 10000.0
EPS = 1e-6
NEG_BIG = -1e30
LOG2_E = 1.4426950408889634
ATTN_Q_SCALE = ATTN_HEAD_DIM ** -0.5 * LOG2_E

LANES = 128
KV_DUP_WIDTH = 2 * LANES
FF_CHUNK = 256
PROJ_CHUNK = 512
ROW_GROUP = 256
FFN_TOKEN_TILE = 1024
PROJ_TOKEN_TILE = 1024
OUT_TOKEN_TILE = 1024
SEQ_TILE = 1024
ROPE_BLOCK = 128
VMEM_LIMIT = 56 * 1024 * 1024

BF16 = jnp.bfloat16
F32 = jnp.float32


def _dot(a, b):
    return jnp.dot(a, b, preferred_element_type=F32)


def _dot_nt(a, b):
    return lax.dot_general(a, b, (((1,), (1,)), ((), ())), preferred_element_type=F32)


def _dot_tn(a, b):
    return lax.dot_general(a, b, (((0,), (0,)), ((), ())), preferred_element_type=F32)


def _rmsnorm(x, g):
    return x * lax.rsqrt(jnp.mean(x * x, axis=-1, keepdims=True) + EPS) * g


def _silu(x):
    return x * (1.0 / (1.0 + jnp.exp(-x)))


def _row_groups(rows):
    return [slice(i, i + ROW_GROUP) for i in range(0, rows, ROW_GROUP)]


def _gate_up(xn, rows, wg_ref, wu_ref, act_ref):
    for c in range(D_FF // FF_CHUNK):
        cols = slice(c * FF_CHUNK, (c + 1) * FF_CHUNK)
        g = _dot(xn, wg_ref[:, cols])
        u = _dot(xn, wu_ref[:, cols])
        act_ref[rows, cols] = (_silu(g) * u).astype(BF16)


def _ffn_kernel(x_ref, g1_ref, wg_ref, wu_ref, wd_ref, gm_ref, h_ref, un_ref, act_ref):
    groups = _row_groups(x_ref.shape[0])
    xns = [_rmsnorm(x_ref[r, :], g1_ref[...]).astype(BF16) for r in groups]
    for r, xn in zip(groups, xns):
        _gate_up(xn, r, wg_ref, wu_ref, act_ref)
    ys = [_dot(act_ref[r, :], wd_ref[...]) for r in groups]
    for r, y in zip(groups, ys):
        h = x_ref[r, :] + 0.5 * y
        h_ref[r, :] = h
        un_ref[r, :] = _rmsnorm(h, gm_ref[...]).astype(BF16)


def _proj_kernel(tiles_per_seq, ldf_ref, ldb_ref, un_ref, win_ref, rope_ref,
                 q_ref, k_ref, v_ref, rq_ref, rk_ref, rkf_ref, rv_ref, gate_ref, sb_ref,
                 decay_ref, rkb_ref, state_ref, kv_ref):
    @pl.when(pl.program_id(0) == 0)
    def _():
        shape = decay_ref.shape[1:]
        j = (lax.broadcasted_iota(jnp.int32, shape, 0) % RET_CHUNK).astype(F32)
        decay_ref[0] = jnp.exp(_head_lane_values(ldf_ref) * (float(RET_CHUNK) - 1.0 - j))
        decay_ref[1] = jnp.exp(_head_lane_values(ldb_ref) * j)

    @pl.when(pl.program_id(0) % tiles_per_seq == 0)
    def _():
        state_ref[...] = jnp.zeros_like(state_ref)

    un = un_ref[...]
    cosa, sina, sinb, cosr, sinr = (rope_ref[:, t * LANES:(t + 1) * LANES] for t in range(5))

    def rope_a(xb):
        return (xb * cosa + pltpu.roll(xb, LANES - ROT_DIM // 2, 1) * sina
                + pltpu.roll(xb, ROT_DIM // 2, 1) * sinb)

    def rope_r(xb):
        return xb * cosr + pltpu.roll(xb, RET_HEAD_DIM // 2, 1) * sinr

    def lane_blocks(p):
        return [p[:, c * LANES:(c + 1) * LANES] for c in range(p.shape[1] // LANES)]

    def proj(section):
        kv_cols = 2 * N_KV_HEADS * ATTN_HEAD_DIM
        if section == 0:
            cols = slice(0, ATTN_WIDTH)
        elif section == 1:
            cols = slice(ATTN_WIDTH, ATTN_WIDTH + kv_cols)
        else:
            start = ATTN_WIDTH + kv_cols + (section - 2) * PROJ_CHUNK
            cols = slice(start, start + PROJ_CHUNK)
        return _dot(un, win_ref[:, cols])

    for c, blk in enumerate(lane_blocks(proj(3))):
        cols = slice(c * LANES, (c + 1) * LANES)
        rk = rope_r(blk) * (RET_HEAD_DIM ** -0.5)
        rk_ref[:, cols] = rk.astype(BF16)
        rkf_ref[:, cols] = (rk * decay_ref[0, :, cols]).astype(BF16)
        rkb_ref[:, cols] = (rk * decay_ref[1, :, cols]).astype(BF16)
    rv_ref[...] = proj(4).astype(BF16)
    gate_ref[...] = _silu(proj(5)).astype(BF16)

    nchunk = un_ref.shape[0] // RET_CHUNK
    for c in range(nchunk):
        for h in range(N_RET_HEADS):
            kv_ref[c, :, _head_cols(h)] = _dot_tn(rkb_ref[_chunk_rows(c), _head_cols(h)],
                                                  rv_ref[_chunk_rows(c), _head_cols(h)])
    chunk_decay = _chunk_decay(ldb_ref)
    state = state_ref[...]
    for c in reversed(range(nchunk)):
        sb_ref[c] = state.astype(BF16)
        state = state * chunk_decay + kv_ref[c]
    state_ref[...] = state

    for c, blk in enumerate(lane_blocks(proj(0))):
        q_ref[:, c * LANES:(c + 1) * LANES] = (rope_a(blk) * ATTN_Q_SCALE).astype(BF16)
    for c, blk in enumerate(lane_blocks(proj(2))):
        rq_ref[:, c * LANES:(c + 1) * LANES] = rope_r(blk).astype(BF16)
    kblk, vblk = lane_blocks(proj(1))
    lo_lanes = lax.broadcasted_iota(jnp.int32, kblk.shape, 1) < ATTN_HEAD_DIM
    for ref, blk in ((k_ref, rope_a(kblk)), (v_ref, vblk)):
        swapped = pltpu.roll(blk, ATTN_HEAD_DIM, 1)
        ref[:, 0:LANES] = jnp.where(lo_lanes, blk, swapped).astype(BF16)
        ref[:, LANES:2 * LANES] = jnp.where(lo_lanes, swapped, blk).astype(BF16)


def _const_spec(shape):
    return pl.BlockSpec(shape, lambda *_: (0,) * len(shape), pipeline_mode=pl.Buffered(1))


def _ffn(x2d, g1, wg, wu, wd, gm):
    n = x2d.shape[0]
    tm = min(FFN_TOKEN_TILE, n)
    row = lambda i: (i, 0)
    return pl.pallas_call(
        _ffn_kernel,
        grid=(n // tm,),
        in_specs=[
            pl.BlockSpec((tm, D_MODEL), row),
            _const_spec((1, D_MODEL)),
            _const_spec((D_MODEL, D_FF)),
            _const_spec((D_MODEL, D_FF)),
            _const_spec((D_FF, D_MODEL)),
            _const_spec((1, D_MODEL)),
        ],
        out_specs=[pl.BlockSpec((tm, D_MODEL), row), pl.BlockSpec((tm, D_MODEL), row)],
        out_shape=[jax.ShapeDtypeStruct((n, D_MODEL), F32), jax.ShapeDtypeStruct((n, D_MODEL), BF16)],
        scratch_shapes=[pltpu.VMEM((tm, D_FF), BF16)],
        compiler_params=pltpu.CompilerParams(dimension_semantics=("parallel",),
                                             vmem_limit_bytes=VMEM_LIMIT),
        name="ffn",
    )(x2d, g1, wg, wu, wd, gm)


def _proj(un, seq_len, win, tables, ld_f, ld_b):
    n = un.shape[0]
    tm = min(PROJ_TOKEN_TILE, seq_len)
    assert tm % RET_CHUNK == 0
    tiles_per_seq = seq_len // tm
    nt = n // tm
    cpt = tm // RET_CHUNK
    row = lambda i: (nt - 1 - i, 0)
    tab = lambda i: ((nt - 1 - i) % tiles_per_seq, 0)
    out_widths = (ATTN_WIDTH, KV_DUP_WIDTH, KV_DUP_WIDTH) + (RET_WIDTH,) * 5
    return pl.pallas_call(
        functools.partial(_proj_kernel, tiles_per_seq),
        grid=(nt,),
        in_specs=[
            pl.BlockSpec(memory_space=pltpu.SMEM),
            pl.BlockSpec(memory_space=pltpu.SMEM),
            pl.BlockSpec((tm, D_MODEL), row),
            _const_spec(win.shape),
            pl.BlockSpec((tm, tables.shape[1]), tab),
        ],
        out_specs=[pl.BlockSpec((tm, w), row) for w in out_widths]
                  + [pl.BlockSpec((cpt, RET_HEAD_DIM, RET_WIDTH), lambda i: (nt - 1 - i, 0, 0))],
        out_shape=[jax.ShapeDtypeStruct((n, w), BF16) for w in out_widths]
                  + [jax.ShapeDtypeStruct((n // RET_CHUNK, RET_HEAD_DIM, RET_WIDTH), BF16)],
        scratch_shapes=[pltpu.VMEM((2, tm, RET_WIDTH), F32),
                        pltpu.VMEM((tm, RET_WIDTH), BF16),
                        pltpu.VMEM((RET_HEAD_DIM, RET_WIDTH), F32),
                        pltpu.VMEM((cpt, RET_HEAD_DIM, RET_WIDTH), F32)],
        compiler_params=pltpu.CompilerParams(dimension_semantics=("arbitrary",),
                                             vmem_limit_bytes=VMEM_LIMIT),
        name="proj",
    )(ld_f, ld_b, un, win, tables)


def _attn_kernel(sink_ref, q_ref, kp_ref, km_ref, kn_ref, vp_ref, vm_ref, vn_ref, gain_ref,
                 o_ref, klo_ref, khi_ref, vlo_ref, vhi_ref, bias_ref, s_ref, p_ref, e_ref, acc_ref):
    t = pl.program_id(1)
    nt = pl.num_programs(1)
    ts = q_ref.shape[1]
    nblk = ts // ATTN_BLOCK
    wk = 3 * ATTN_BLOCK

    @pl.when(t == 0)
    def _():
        qi = lax.broadcasted_iota(jnp.int32, (ATTN_BLOCK, wk), 0)
        kj = lax.broadcasted_iota(jnp.int32, (ATTN_BLOCK, wk), 1)
        band = (kj >= qi) & (kj <= qi + 2 * WINDOW)
        bias_ref[0] = jnp.where(band, 0.0, NEG_BIG)
        bias_ref[1] = jnp.where(band & (kj >= ATTN_BLOCK), 0.0, NEG_BIG)
        bias_ref[2] = jnp.where(band & (kj < 2 * ATTN_BLOCK), 0.0, NEG_BIG)

    lo_lanes = (lax.broadcasted_iota(jnp.int32, (1, KV_DUP_WIDTH), 1) % LANES) < ATTN_HEAD_DIM
    window = ((slice(0, ATTN_BLOCK), kp_ref, vp_ref), (slice(ATTN_BLOCK, ATTN_BLOCK + ts), km_ref, vm_ref),
              (slice(ATTN_BLOCK + ts, 2 * ATTN_BLOCK + ts), kn_ref, vn_ref))
    for dst, k_src, v_src in window:
        kblk, vblk = k_src[0], v_src[0]
        klo_ref[dst, :] = jnp.where(lo_lanes, kblk, jnp.zeros_like(kblk))
        khi_ref[dst, :] = jnp.where(lo_lanes, jnp.zeros_like(kblk), kblk)
        vlo_ref[dst, :] = jnp.where(lo_lanes, vblk, jnp.ones_like(vblk))
        vhi_ref[dst, :] = jnp.where(lo_lanes, jnp.ones_like(vblk), vblk)

    out_lo = lax.broadcasted_iota(jnp.int32, (ATTN_BLOCK, LANES), 1) < ATTN_HEAD_DIM
    first_idx = jnp.where(t == 0, 1, 0)
    last_idx = jnp.where(t == nt - 1, 2, 0)

    units = [(j, g, pair, half) for j in range(nblk) for g in range(N_KV_HEADS)
             for pair in range(2) for half in range(2)]

    def rows_of(j):
        return slice(j * ATTN_BLOCK, (j + 1) * ATTN_BLOCK)

    def win_rows(j):
        return slice(j * ATTN_BLOCK, j * ATTN_BLOCK + wk)

    def pair_cols(g, pair):
        return slice((2 * g + pair) * LANES, (2 * g + pair + 1) * LANES)

    for u, (j, g, pair, half) in enumerate(units):
        k_ref = khi_ref if half else klo_ref
        bias_idx = first_idx if j == 0 else (last_idx if j == nblk - 1 else 0)
        s = _dot_nt(q_ref[0, rows_of(j), pair_cols(g, pair)],
                    k_ref[win_rows(j), g * LANES:(g + 1) * LANES])
        for blk in range(3):
            cols = slice(blk * ATTN_BLOCK, (blk + 1) * ATTN_BLOCK)
            s_ref[u, :, cols] = s[:, cols] if blk == 1 else s[:, cols] + bias_ref[bias_idx, :, cols]

    for u, (j, g, pair, half) in enumerate(units):
        s = s_ref[u]
        sink = sink_ref[4 * g + 2 * pair + half] * LOG2_E
        m = jnp.maximum(jnp.max(s, axis=-1, keepdims=True), sink)
        p_ref[u] = jnp.exp2(s - m).astype(BF16)
        e_ref[u] = jnp.broadcast_to(jnp.exp2(sink - m), (ATTN_BLOCK, LANES))

    for u in range(0, len(units), 2):
        j, g, pair, _ = units[u]
        kv_cols = slice(g * LANES, (g + 1) * LANES)
        even = _dot(p_ref[u], vlo_ref[win_rows(j), kv_cols])
        odd = _dot(p_ref[u + 1], vhi_ref[win_rows(j), kv_cols])
        numer = jnp.where(out_lo, even, odd)
        rowsum = pltpu.roll(jnp.where(out_lo, odd, even), ATTN_HEAD_DIM, 1)
        denom = rowsum + jnp.where(out_lo, e_ref[u], e_ref[u + 1])
        acc_ref[rows_of(j), pair_cols(g, pair)] = numer * (1.0 / denom)

    o_ref[0] = _rmsnorm(acc_ref[...], gain_ref[...]).astype(BF16)


def _attn_specs(b, s, ts):
    bps = ts // ATTN_BLOCK
    nb = s // ATTN_BLOCK
    assert bps >= 2
    main = lambda i, t: (i, t, 0)
    prev = lambda i, t: (i, jnp.maximum(t * bps - 1, 0), 0)
    nxt = lambda i, t: (i, jnp.minimum((t + 1) * bps, nb - 1), 0)
    kv_specs = [pl.BlockSpec((1, ATTN_BLOCK, KV_DUP_WIDTH), prev),
                pl.BlockSpec((1, ts, KV_DUP_WIDTH), main),
                pl.BlockSpec((1, ATTN_BLOCK, KV_DUP_WIDTH), nxt)]
    in_specs = ([pl.BlockSpec(memory_space=pltpu.SMEM), pl.BlockSpec((1, ts, ATTN_WIDTH), main)]
                + kv_specs + kv_specs + [pl.BlockSpec((1, ATTN_WIDTH), lambda i, t: (0, 0))])
    scratch = [pltpu.VMEM((ts + 2 * ATTN_BLOCK, KV_DUP_WIDTH), BF16)] * 4 + [
        pltpu.VMEM((3, ATTN_BLOCK, 3 * ATTN_BLOCK), F32),
        pltpu.VMEM((bps * N_ATTN_HEADS, ATTN_BLOCK, 3 * ATTN_BLOCK), F32),
        pltpu.VMEM((bps * N_ATTN_HEADS, ATTN_BLOCK, 3 * ATTN_BLOCK), BF16),
        pltpu.VMEM((bps * N_ATTN_HEADS, ATTN_BLOCK, LANES), F32),
        pltpu.VMEM((ts, ATTN_WIDTH), F32)]
    return (in_specs, pl.BlockSpec((1, ts, ATTN_WIDTH), main),
            jax.ShapeDtypeStruct((b, s, ATTN_WIDTH), BF16), scratch)


def _head_lane_values(ld_ref):
    lane = lax.broadcasted_iota(jnp.int32, (1, RET_WIDTH), 1)
    ld = jnp.zeros((1, RET_WIDTH), F32)
    for h in range(N_RET_HEADS):
        ld = jnp.where(lane >= h * RET_HEAD_DIM, ld_ref[h], ld)
    return ld


def _chunk_decay(ld_ref):
    return jnp.exp(_head_lane_values(ld_ref) * float(RET_CHUNK))


def _head_cols(h):
    return slice(h * RET_HEAD_DIM, (h + 1) * RET_HEAD_DIM)


def _chunk_rows(c):
    return slice(c * RET_CHUNK, (c + 1) * RET_CHUNK)


def _ret_out_kernel(ldf_ref, ldb_ref, rq_ref, rk_ref, rkf_ref, rv_ref, gate_ref, sb_ref, o_ref,
                    state_ref, tab_ref, inner_ref, kv_ref, cs_ref):
    t = pl.program_id(1)
    nchunk = rq_ref.shape[1] // RET_CHUNK
    MASK, Q_FWD, Q_BWD = range(3)

    @pl.when(t == 0)
    def _():
        state_ref[...] = jnp.zeros_like(state_ref)
        shape = (RET_CHUNK, RET_HEAD_DIM)
        row_i = lax.broadcasted_iota(jnp.int32, shape, 0)
        col_i = lax.broadcasted_iota(jnp.int32, shape, 1)
        row = row_i.astype(F32)
        diff = (row_i - col_i).astype(F32)
        for h in range(N_RET_HEADS):
            d_fwd = jnp.where(diff >= 0, jnp.exp(ldf_ref[h] * jnp.maximum(diff, 0.0)), 0.0)
            d_bwd = jnp.where(diff <= 0, jnp.exp(ldb_ref[h] * jnp.maximum(-diff, 0.0)), 0.0)
            tab_ref[MASK, h] = d_fwd + d_bwd
            tab_ref[Q_FWD, h] = jnp.exp(ldf_ref[h] * (row + 1.0))
            tab_ref[Q_BWD, h] = jnp.exp(ldb_ref[h] * (float(RET_CHUNK) - row))

    units = [(c, h) for c in range(nchunk) for h in range(N_RET_HEADS)]

    for u, (c, h) in enumerate(units):
        q = rq_ref[0, _chunk_rows(c), _head_cols(h)]
        k = rk_ref[0, _chunk_rows(c), _head_cols(h)]
        inner_ref[u] = (_dot_nt(q, k) * tab_ref[MASK, h]).astype(BF16)
        kv_ref[c, :, _head_cols(h)] = _dot_tn(rkf_ref[0, _chunk_rows(c), _head_cols(h)],
                                              rv_ref[0, _chunk_rows(c), _head_cols(h)])

    chunk_decay = _chunk_decay(ldf_ref)
    state = state_ref[...]
    for c in range(nchunk):
        fwd = state.astype(BF16)
        for h in range(N_RET_HEADS):
            base = 2 * h * RET_HEAD_DIM
            cs_ref[c, :, base:base + RET_HEAD_DIM] = fwd[:, _head_cols(h)]
            cs_ref[c, :, base + RET_HEAD_DIM:base + 2 * RET_HEAD_DIM] = sb_ref[0, c, :, _head_cols(h)]
        state = state * chunk_decay + kv_ref[c]
    state_ref[...] = state

    for u, (c, h) in enumerate(units):
        q = rq_ref[0, _chunk_rows(c), _head_cols(h)]
        o = _dot(inner_ref[u], rv_ref[0, _chunk_rows(c), _head_cols(h)])
        cross = _dot(q, cs_ref[c, :, 2 * h * RET_HEAD_DIM:2 * (h + 1) * RET_HEAD_DIM])
        o += tab_ref[Q_FWD, h] * cross[:, :RET_HEAD_DIM] + tab_ref[Q_BWD, h] * cross[:, RET_HEAD_DIM:]
        mu = jnp.mean(o, axis=-1, keepdims=True)
        var = jnp.mean(jnp.square(o - mu), axis=-1, keepdims=True)
        on = (o - mu) * lax.rsqrt(var + EPS)
        gate = gate_ref[0, _chunk_rows(c), _head_cols(h)].astype(F32)
        o_ref[0, _chunk_rows(c), _head_cols(h)] = (gate * on).astype(BF16)


def _ret_out_specs(b, s, ts):
    cpt = ts // RET_CHUNK
    seq_spec = pl.BlockSpec((1, ts, RET_WIDTH), lambda i, t: (i, t, 0))
    in_specs = [pl.BlockSpec(memory_space=pltpu.SMEM), pl.BlockSpec(memory_space=pltpu.SMEM),
                seq_spec, seq_spec, seq_spec, seq_spec, seq_spec,
                pl.BlockSpec((1, cpt, RET_HEAD_DIM, RET_WIDTH), lambda i, t: (i, t, 0, 0))]
    scratch = [pltpu.VMEM((RET_HEAD_DIM, RET_WIDTH), F32),
               pltpu.VMEM((3, N_RET_HEADS, RET_CHUNK, RET_HEAD_DIM), F32),
               pltpu.VMEM((cpt * N_RET_HEADS, RET_CHUNK, RET_CHUNK), BF16),
               pltpu.VMEM((cpt, RET_HEAD_DIM, RET_WIDTH), F32),
               pltpu.VMEM((cpt, RET_HEAD_DIM, 2 * RET_WIDTH), BF16)]
    return in_specs, seq_spec, jax.ShapeDtypeStruct((b, s, RET_WIDTH), BF16), scratch


def _mix(attn_args, ret_args):
    sink, q, k, v, gain = attn_args
    b, s, _ = q.shape
    ts = min(SEQ_TILE, s)
    a_in, a_out, a_shape, a_scratch = _attn_specs(b, s, ts)
    r_in, r_out, r_shape, r_scratch = _ret_out_specs(b, s, ts)
    n_a_in, n_r_in, n_a_scr = len(a_in), len(r_in), len(a_scratch)

    def mix_kernel(*refs):
        ins, (attn_o, ret_o), scr = refs[:n_a_in + n_r_in], refs[n_a_in + n_r_in:][:2], refs[n_a_in + n_r_in + 2:]
        _ret_out_kernel(*ins[n_a_in:], ret_o, *scr[n_a_scr:])
        _attn_kernel(*ins[:n_a_in], attn_o, *scr[:n_a_scr])

    return pl.pallas_call(
        mix_kernel,
        grid=(b, s // ts),
        in_specs=a_in + r_in,
        out_specs=[a_out, r_out],
        out_shape=[a_shape, r_shape],
        scratch_shapes=a_scratch + r_scratch,
        compiler_params=pltpu.CompilerParams(dimension_semantics=("parallel", "arbitrary"),
                                             vmem_limit_bytes=VMEM_LIMIT),
        name="mix",
    )(sink, q, k, k, k, v, v, v, gain, *ret_args)


def _out_ffn_kernel(h_ref, a_ref, r_ref, woa_ref, wor_ref, g2_ref, wg_ref, wu_ref, wd_ref, gf_ref,
                    y_ref, act_ref):
    groups = _row_groups(h_ref.shape[0])
    hs = [h_ref[r, :] + _dot(a_ref[r, :], woa_ref[...]) + _dot(r_ref[r, :], wor_ref[...]) for r in groups]
    hns = [_rmsnorm(h, g2_ref[...]).astype(BF16) for h in hs]
    for r, hn in zip(groups, hns):
        _gate_up(hn, r, wg_ref, wu_ref, act_ref)
    ys = [_dot(act_ref[r, :], wd_ref[...]) for r in groups]
    for r, h, y in zip(groups, hs, ys):
        y_ref[r, :] = _rmsnorm(h + 0.5 * y, gf_ref[...])


def _out_ffn(h, a, r, woa, wor, g2, wg, wu, wd, gf):
    n = h.shape[0]
    tm = min(OUT_TOKEN_TILE, n)
    row = lambda i: (i, 0)
    return pl.pallas_call(
        _out_ffn_kernel,
        grid=(n // tm,),
        in_specs=[
            pl.BlockSpec((tm, D_MODEL), row),
            pl.BlockSpec((tm, ATTN_WIDTH), row),
            pl.BlockSpec((tm, RET_WIDTH), row),
            _const_spec((ATTN_WIDTH, D_MODEL)),
            _const_spec((RET_WIDTH, D_MODEL)),
            _const_spec((1, D_MODEL)),
            _const_spec((D_MODEL, D_FF)),
            _const_spec((D_MODEL, D_FF)),
            _const_spec((D_FF, D_MODEL)),
            _const_spec((1, D_MODEL)),
        ],
        out_specs=pl.BlockSpec((tm, D_MODEL), row),
        out_shape=jax.ShapeDtypeStruct((n, D_MODEL), F32),
        scratch_shapes=[pltpu.VMEM((tm, D_FF), BF16)],
        compiler_params=pltpu.CompilerParams(dimension_semantics=("parallel",),
                                             vmem_limit_bytes=VMEM_LIMIT),
        name="out_ffn",
    )(h, a, r, woa, wor, g2, wg, wu, wd, gf)


def _rope_tables(seq_len):
    half = ROT_DIM // 2
    lane = jnp.arange(LANES) % ATTN_HEAD_DIM
    inv_a = ROPE_THETA ** (-jnp.arange(0, ROT_DIM, 2, dtype=F32) / ROT_DIM)
    freq_a = jnp.where(lane < ROT_DIM, inv_a[lane % half], 0.0)
    inv_r = RET_ROT_THETA ** (-jnp.linspace(0.0, 1.0, RET_HEAD_DIM // 2, dtype=F32))
    freq_r = jnp.concatenate([inv_r, inv_r])

    start = (jnp.arange(seq_len // ROPE_BLOCK, dtype=F32) * ROPE_BLOCK)[:, None, None]
    offset = jnp.arange(ROPE_BLOCK, dtype=F32)[None, :, None]

    def cos_sin(freq):
        a, b = start * freq, offset * freq
        cos = jnp.cos(a) * jnp.cos(b) - jnp.sin(a) * jnp.sin(b)
        sin = jnp.sin(a) * jnp.cos(b) + jnp.cos(a) * jnp.sin(b)
        return cos.reshape(seq_len, LANES), sin.reshape(seq_len, LANES)

    cos_a, sin = cos_sin(freq_a)
    sin_a = sin * jnp.where(lane < half, -1.0, 0.0)[None, :]
    sin_b = sin * jnp.where((lane >= half) & (lane < ROT_DIM), 1.0, 0.0)[None, :]
    cos_r, sin = cos_sin(freq_r)
    sign_r = jnp.where(jnp.arange(LANES) < RET_HEAD_DIM // 2, -1.0, 1.0)
    return jnp.concatenate([cos_a, sin_a, sin_b, cos_r, sin * sign_r[None, :]], axis=1)


def _layer(x, p, tables):
    b, s, _ = x.shape
    n = b * s
    h, un = _ffn(x.reshape(n, D_MODEL), p["g1"], p["wg1"], p["wu1"], p["wd1"], p["gm"])
    q, k, v, rq, rk, rkf, rv, gate, sb = _proj(un, s, p["win"], tables, p["ldf"], p["ldb"])
    seq = lambda a: a.reshape(b, s, a.shape[-1])
    sb = sb.reshape(b, s // RET_CHUNK, RET_HEAD_DIM, RET_WIDTH)
    attn, ret = _mix((p["sink"], seq(q), seq(k), seq(v), p["ga"]),
                     (p["ldf"], p["ldb"], seq(rq), seq(rk), seq(rkf), seq(rv), seq(gate), sb))
    y = _out_ffn(h, attn.reshape(n, ATTN_WIDTH), ret.reshape(n, RET_WIDTH),
                 p["woa"], p["wor"], p["g2"], p["wg2"], p["wu2"], p["wd2"], p["gf"])
    return y.reshape(b, s, D_MODEL)


def kernel(x_prompt, x_sample, ffn1_norm, ffn1_w_gate, ffn1_w_up, ffn1_w_down, mix_norm, w_in, attn_sink, attn_out_norm, ret_log_decay_fwd, ret_log_decay_bwd, w_out, ffn2_norm, ffn2_w_gate, ffn2_w_up, ffn2_w_down, final_norm):
    assert ffn1_norm.shape[0] == 1
    wo = w_out[0].astype(BF16)
    p = dict(
        g1=ffn1_norm[0][None, :], wg1=ffn1_w_gate[0].astype(BF16), wu1=ffn1_w_up[0].astype(BF16),
        wd1=ffn1_w_down[0].astype(BF16), gm=mix_norm[0][None, :],
        win=w_in[0].astype(BF16),
        sink=attn_sink[0], ga=attn_out_norm[0][None, :],
        ldf=ret_log_decay_fwd[0], ldb=ret_log_decay_bwd[0],
        woa=wo[:ATTN_WIDTH], wor=wo[ATTN_WIDTH:],
        g2=ffn2_norm[0][None, :], wg2=ffn2_w_gate[0].astype(BF16), wu2=ffn2_w_up[0].astype(BF16),
        wd2=ffn2_w_down[0].astype(BF16), gf=final_norm[None, :],
    )
    tables = _rope_tables(max(x_prompt.shape[1], x_sample.shape[1]))
    return (_layer(x_prompt, p, tables), _layer(x_sample, p, tables))
```

```python
import functools

import jax
import jax.numpy as jnp
from jax import lax
from jax.experimental import pallas as pl
from jax.experimental.pallas import tpu as pltpu

D_MODEL = 1024
D_FF = 2816
N_ATTN_HEADS = 8
N_KV_HEADS = 2
ATTN_HEAD_DIM = 64
ATTN_WIDTH = N_ATTN_HEADS * ATTN_HEAD_DIM
WINDOW = 128
ATTN_BLOCK = 128
ROT_DIM = 16
ROPE_THETA = 500000.0
RET_HEAD_DIM = 128
N_RET_HEADS = 4
RET_WIDTH = N_RET_HEADS * RET_HEAD_DIM
RET_CHUNK = 128
RET_ROT_THETA = 10000.0
EPS = 1e-6
NEG_BIG = -1e30
LOG2_E = 1.4426950408889634
ATTN_Q_SCALE = ATTN_HEAD_DIM ** -0.5 * LOG2_E

LANES = 128
KV_DUP_WIDTH = 2 * LANES
FF_CHUNK = 256
PROJ_CHUNK = 512
ROW_GROUP = 256
FFN_TOKEN_TILE = 1024
PROJ_TOKEN_TILE = 1024
OUT_TOKEN_TILE = 1024
SEQ_TILE = 1024
ROPE_BLOCK = 128
VMEM_LIMIT = 56 * 1024 * 1024

BF16 = jnp.bfloat16
F32 = jnp.float32


def _dot(a, b):
    return jnp.dot(a, b, preferred_element_type=F32)


def _dot_nt(a, b):
    return lax.dot_general(a, b, (((1,), (1,)), ((), ())), preferred_element_type=F32)


def _dot_tn(a, b):
    return lax.dot_general(a, b, (((0,), (0,)), ((), ())), preferred_element_type=F32)


def _rmsnorm(x, g):
    return x * lax.rsqrt(jnp.mean(x * x, axis=-1, keepdims=True) + EPS) * g


def _silu(x):
    return x * (1.0 / (1.0 + jnp.exp(-x)))


def _row_groups(rows):
    return [slice(i, i + ROW_GROUP) for i in range(0, rows, ROW_GROUP)]


def _gate_up(xn, rows, wg_ref, wu_ref, act_ref):
    for c in range(D_FF // FF_CHUNK):
        cols = slice(c * FF_CHUNK, (c + 1) * FF_CHUNK)
        g = _dot(xn, wg_ref[:, cols])
        u = _dot(xn, wu_ref[:, cols])
        act_ref[rows, cols] = (_silu(g) * u).astype(BF16)


def _ffn_kernel(x_ref, g1_ref, wg_ref, wu_ref, wd_ref, gm_ref, h_ref, un_ref, act_ref):
    groups = _row_groups(x_ref.shape[0])
    xns = [_rmsnorm(x_ref[r, :], g1_ref[...]).astype(BF16) for r in groups]
    for r, xn in zip(groups, xns):
        _gate_up(xn, r, wg_ref, wu_ref, act_ref)
    ys = [_dot(act_ref[r, :], wd_ref[...]) for r in groups]
    for r, y in zip(groups, ys):
        h = x_ref[r, :] + 0.5 * y
        h_ref[r, :] = h
        un_ref[r, :] = _rmsnorm(h, gm_ref[...]).astype(BF16)


def _proj_kernel(tiles_per_seq, ldf_ref, ldb_ref, un_ref, win_ref, rope_ref,
                 q_ref, k_ref, v_ref, rq_ref, rk_ref, rkf_ref, rv_ref, gate_ref, sb_ref,
                 decay_ref, rkb_ref, state_ref, kv_ref):
    @pl.when(pl.program_id(0) == 0)
    def _():
        shape = decay_ref.shape[1:]
        j = (lax.broadcasted_iota(jnp.int32, shape, 0) % RET_CHUNK).astype(F32)
        decay_ref[0] = jnp.exp(_head_lane_values(ldf_ref) * (float(RET_CHUNK) - 1.0 - j))
        decay_ref[1] = jnp.exp(_head_lane_values(ldb_ref) * j)

    @pl.when(pl.program_id(0) % tiles_per_seq == 0)
    def _():
        state_ref[...] = jnp.zeros_like(state_ref)

    un = un_ref[...]
    cosa, sina, sinb, cosr, sinr = (rope_ref[:, t * LANES:(t + 1) * LANES] for t in range(5))

    def rope_a(xb):
        return (xb * cosa + pltpu.roll(xb, LANES - ROT_DIM // 2, 1) * sina
                + pltpu.roll(xb, ROT_DIM // 2, 1) * sinb)

    def rope_r(xb):
        return xb * cosr + pltpu.roll(xb, RET_HEAD_DIM // 2, 1) * sinr

    def lane_blocks(p):
        return [p[:, c * LANES:(c + 1) * LANES] for c in range(p.shape[1] // LANES)]

    def proj(section):
        kv_cols = 2 * N_KV_HEADS * ATTN_HEAD_DIM
        if section == 0:
            cols = slice(0, ATTN_WIDTH)
        elif section == 1:
            cols = slice(ATTN_WIDTH, ATTN_WIDTH + kv_cols)
        else:
            start = ATTN_WIDTH + kv_cols + (section - 2) * PROJ_CHUNK
            cols = slice(start, start + PROJ_CHUNK)
        return _dot(un, win_ref[:, cols])

    for c, blk in enumerate(lane_blocks(proj(3))):
        cols = slice(c * LANES, (c + 1) * LANES)
        rk = rope_r(blk) * (RET_HEAD_DIM ** -0.5)
        rk_ref[:, cols] = rk.astype(BF16)
        rkf_ref[:, cols] = (rk * decay_ref[0, :, cols]).astype(BF16)
        rkb_ref[:, cols] = (rk * decay_ref[1, :, cols]).astype(BF16)
    rv_ref[...] = proj(4).astype(BF16)
    gate_ref[...] = _silu(proj(5)).astype(BF16)

    nchunk = un_ref.shape[0] // RET_CHUNK
    for c in range(nchunk):
        for h in range(N_RET_HEADS):
            kv_ref[c, :, _head_cols(h)] = _dot_tn(rkb_ref[_chunk_rows(c), _head_cols(h)],
                                                  rv_ref[_chunk_rows(c), _head_cols(h)])
    chunk_decay = _chunk_decay(ldb_ref)
    state = state_ref[...]
    for c in reversed(range(nchunk)):
        sb_ref[c] = state.astype(BF16)
        state = state * chunk_decay + kv_ref[c]
    state_ref[...] = state

    for c, blk in enumerate(lane_blocks(proj(0))):
        q_ref[:, c * LANES:(c + 1) * LANES] = (rope_a(blk) * ATTN_Q_SCALE).astype(BF16)
    for c, blk in enumerate(lane_blocks(proj(2))):
        rq_ref[:, c * LANES:(c + 1) * LANES] = rope_r(blk).astype(BF16)
    kblk, vblk = lane_blocks(proj(1))
    lo_lanes = lax.broadcasted_iota(jnp.int32, kblk.shape, 1) < ATTN_HEAD_DIM
    for ref, blk in ((k_ref, rope_a(kblk)), (v_ref, vblk)):
        swapped = pltpu.roll(blk, ATTN_HEAD_DIM, 1)
        ref[:, 0:LANES] = jnp.where(lo_lanes, blk, swapped).astype(BF16)
        ref[:, LANES:2 * LANES] = jnp.where(lo_lanes, swapped, blk).astype(BF16)


def _const_spec(shape):
    return pl.BlockSpec(shape, lambda *_: (0,) * len(shape), pipeline_mode=pl.Buffered(1))


def _ffn(x2d, g1, wg, wu, wd, gm):
    n = x2d.shape[0]
    tm = min(FFN_TOKEN_TILE, n)
    row = lambda i: (i, 0)
    return pl.pallas_call(
        _ffn_kernel,
        grid=(n // tm,),
        in_specs=[
            pl.BlockSpec((tm, D_MODEL), row),
            _const_spec((1, D_MODEL)),
            _const_spec((D_MODEL, D_FF)),
            _const_spec((D_MODEL, D_FF)),
            _const_spec((D_FF, D_MODEL)),
            _const_spec((1, D_MODEL)),
        ],
        out_specs=[pl.BlockSpec((tm, D_MODEL), row), pl.BlockSpec((tm, D_MODEL), row)],
        out_shape=[jax.ShapeDtypeStruct((n, D_MODEL), F32), jax.ShapeDtypeStruct((n, D_MODEL), BF16)],
        scratch_shapes=[pltpu.VMEM((tm, D_FF), BF16)],
        compiler_params=pltpu.CompilerParams(dimension_semantics=("parallel",),
                                             vmem_limit_bytes=VMEM_LIMIT),
        name="ffn",
    )(x2d, g1, wg, wu, wd, gm)


def _proj(un, seq_len, win, tables, ld_f, ld_b):
    n = un.shape[0]
    tm = min(PROJ_TOKEN_TILE, seq_len)
    assert tm % RET_CHUNK == 0
    tiles_per_seq = seq_len // tm
    nt = n // tm
    cpt = tm // RET_CHUNK
    row = lambda i: (nt - 1 - i, 0)
    tab = lambda i: ((nt - 1 - i) % tiles_per_seq, 0)
    out_widths = (ATTN_WIDTH, KV_DUP_WIDTH, KV_DUP_WIDTH) + (RET_WIDTH,) * 5
    return pl.pallas_call(
        functools.partial(_proj_kernel, tiles_per_seq),
        grid=(nt,),
        in_specs=[
            pl.BlockSpec(memory_space=pltpu.SMEM),
            pl.BlockSpec(memory_space=pltpu.SMEM),
            pl.BlockSpec((tm, D_MODEL), row),
            _const_spec(win.shape),
            pl.BlockSpec((tm, tables.shape[1]), tab),
        ],
        out_specs=[pl.BlockSpec((tm, w), row) for w in out_widths]
                  + [pl.BlockSpec((cpt, RET_HEAD_DIM, RET_WIDTH), lambda i: (nt - 1 - i, 0, 0))],
        out_shape=[jax.ShapeDtypeStruct((n, w), BF16) for w in out_widths]
                  + [jax.ShapeDtypeStruct((n // RET_CHUNK, RET_HEAD_DIM, RET_WIDTH), BF16)],
        scratch_shapes=[pltpu.VMEM((2, tm, RET_WIDTH), F32),
                        pltpu.VMEM((tm, RET_WIDTH), BF16),
                        pltpu.VMEM((RET_HEAD_DIM, RET_WIDTH), F32),
                        pltpu.VMEM((cpt, RET_HEAD_DIM, RET_WIDTH), F32)],
        compiler_params=pltpu.CompilerParams(dimension_semantics=("arbitrary",),
                                             vmem_limit_bytes=VMEM_LIMIT),
        name="proj",
    )(ld_f, ld_b, un, win, tables)


def _attn_kernel(sink_ref, q_ref, kp_ref, km_ref, kn_ref, vp_ref, vm_ref, vn_ref, gain_ref,
                 o_ref, klo_ref, khi_ref, vlo_ref, vhi_ref, bias_ref, s_ref, p_ref, e_ref, acc_ref):
    t = pl.program_id(1)
    nt = pl.num_programs(1)
    ts = q_ref.shape[1]
    nblk = ts // ATTN_BLOCK
    wk = 3 * ATTN_BLOCK

    @pl.when(t == 0)
    def _():
        qi = lax.broadcasted_iota(jnp.int32, (ATTN_BLOCK, wk), 0)
        kj = lax.broadcasted_iota(jnp.int32, (ATTN_BLOCK, wk), 1)
        band = (kj >= qi) & (kj <= qi + 2 * WINDOW)
        bias_ref[0] = jnp.where(band, 0.0, NEG_BIG)
        bias_ref[1] = jnp.where(band & (kj >= ATTN_BLOCK), 0.0, NEG_BIG)
        bias_ref[2] = jnp.where(band & (kj < 2 * ATTN_BLOCK), 0.0, NEG_BIG)

    yield

    lo_lanes = (lax.broadcasted_iota(jnp.int32, (1, KV_DUP_WIDTH), 1) % LANES) < ATTN_HEAD_DIM
    window = ((slice(0, ATTN_BLOCK), kp_ref, vp_ref), (slice(ATTN_BLOCK, ATTN_BLOCK + ts), km_ref, vm_ref),
              (slice(ATTN_BLOCK + ts, 2 * ATTN_BLOCK + ts), kn_ref, vn_ref))
    for dst, k_src, v_src in window:
        kblk, vblk = k_src[0], v_src[0]
        klo_ref[dst, :] = jnp.where(lo_lanes, kblk, jnp.zeros_like(kblk))
        khi_ref[dst, :] = jnp.where(lo_lanes, jnp.zeros_like(kblk), kblk)
        vlo_ref[dst, :] = jnp.where(lo_lanes, vblk, jnp.ones_like(vblk))
        vhi_ref[dst, :] = jnp.where(lo_lanes, jnp.ones_like(vblk), vblk)

    out_lo = lax.broadcasted_iota(jnp.int32, (ATTN_BLOCK, LANES), 1) < ATTN_HEAD_DIM
    first_idx = jnp.where(t == 0, 1, 0)
    last_idx = jnp.where(t == nt - 1, 2, 0)

    units = [(j, g, pair, half) for j in range(nblk) for g in range(N_KV_HEADS)
             for pair in range(2) for half in range(2)]

    def rows_of(j):
        return slice(j * ATTN_BLOCK, (j + 1) * ATTN_BLOCK)

    def win_rows(j):
        return slice(j * ATTN_BLOCK, j * ATTN_BLOCK + wk)

    def pair_cols(g, pair):
        return slice((2 * g + pair) * LANES, (2 * g + pair + 1) * LANES)

    for u, (j, g, pair, half) in enumerate(units):
        k_ref = khi_ref if half else klo_ref
        bias_idx = first_idx if j == 0 else (last_idx if j == nblk - 1 else 0)
        s = _dot_nt(q_ref[0, rows_of(j), pair_cols(g, pair)],
                    k_ref[win_rows(j), g * LANES:(g + 1) * LANES])
        for blk in range(3):
            cols = slice(blk * ATTN_BLOCK, (blk + 1) * ATTN_BLOCK)
            s_ref[u, :, cols] = s[:, cols] if blk == 1 else s[:, cols] + bias_ref[bias_idx, :, cols]

    yield

    for u, (j, g, pair, half) in enumerate(units):
        s = s_ref[u]
        sink = sink_ref[4 * g + 2 * pair + half] * LOG2_E
        m = jnp.maximum(jnp.max(s, axis=-1, keepdims=True), sink)
        p_ref[u] = jnp.exp2(s - m).astype(BF16)
        e_ref[u] = jnp.broadcast_to(jnp.exp2(sink - m), (ATTN_BLOCK, LANES))

    yield

    for u in range(0, len(units), 2):
        j, g, pair, _ = units[u]
        kv_cols = slice(g * LANES, (g + 1) * LANES)
        even = _dot(p_ref[u], vlo_ref[win_rows(j), kv_cols])
        odd = _dot(p_ref[u + 1], vhi_ref[win_rows(j), kv_cols])
        numer = jnp.where(out_lo, even, odd)
        rowsum = pltpu.roll(jnp.where(out_lo, odd, even), ATTN_HEAD_DIM, 1)
        denom = rowsum + jnp.where(out_lo, e_ref[u], e_ref[u + 1])
        acc_ref[rows_of(j), pair_cols(g, pair)] = numer * (1.0 / denom)

    o_ref[0] = _rmsnorm(acc_ref[...], gain_ref[...]).astype(BF16)


def _attn_specs(b, s, ts):
    bps = ts // ATTN_BLOCK
    nb = s // ATTN_BLOCK
    assert bps >= 2
    main = lambda i, t: (i, t, 0)
    prev = lambda i, t: (i, jnp.maximum(t * bps - 1, 0), 0)
    nxt = lambda i, t: (i, jnp.minimum((t + 1) * bps, nb - 1), 0)
    kv_specs = [pl.BlockSpec((1, ATTN_BLOCK, KV_DUP_WIDTH), prev),
                pl.BlockSpec((1, ts, KV_DUP_WIDTH), main),
                pl.BlockSpec((1, ATTN_BLOCK, KV_DUP_WIDTH), nxt)]
    in_specs = ([pl.BlockSpec(memory_space=pltpu.SMEM), pl.BlockSpec((1, ts, ATTN_WIDTH), main)]
                + kv_specs + kv_specs + [pl.BlockSpec((1, ATTN_WIDTH), lambda i, t: (0, 0))])
    scratch = [pltpu.VMEM((ts + 2 * ATTN_BLOCK, KV_DUP_WIDTH), BF16)] * 4 + [
        pltpu.VMEM((3, ATTN_BLOCK, 3 * ATTN_BLOCK), F32),
        pltpu.VMEM((bps * N_ATTN_HEADS, ATTN_BLOCK, 3 * ATTN_BLOCK), F32),
        pltpu.VMEM((bps * N_ATTN_HEADS, ATTN_BLOCK, 3 * ATTN_BLOCK), BF16),
        pltpu.VMEM((bps * N_ATTN_HEADS, ATTN_BLOCK, LANES), F32),
        pltpu.VMEM((ts, ATTN_WIDTH), F32)]
    return (in_specs, pl.BlockSpec((1, ts, ATTN_WIDTH), main),
            jax.ShapeDtypeStruct((b, s, ATTN_WIDTH), BF16), scratch)


def _head_lane_values(ld_ref):
    lane = lax.broadcasted_iota(jnp.int32, (1, RET_WIDTH), 1)
    ld = jnp.zeros((1, RET_WIDTH), F32)
    for h in range(N_RET_HEADS):
        ld = jnp.where(lane >= h * RET_HEAD_DIM, ld_ref[h], ld)
    return ld


def _chunk_decay(ld_ref):
    return jnp.exp(_head_lane_values(ld_ref) * float(RET_CHUNK))


def _head_cols(h):
    return slice(h * RET_HEAD_DIM, (h + 1) * RET_HEAD_DIM)


def _chunk_rows(c):
    return slice(c * RET_CHUNK, (c + 1) * RET_CHUNK)


def _ret_out_kernel(ldf_ref, ldb_ref, rq_ref, rk_ref, rkf_ref, rv_ref, gate_ref, sb_ref, o_ref,
                    state_ref, tab_ref, inner_ref, kv_ref, cs_ref):
    t = pl.program_id(1)
    nchunk = rq_ref.shape[1] // RET_CHUNK
    MASK, Q_FWD, Q_BWD = range(3)

    @pl.when(t == 0)
    def _():
        state_ref[...] = jnp.zeros_like(state_ref)
        shape = (RET_CHUNK, RET_HEAD_DIM)
        row_i = lax.broadcasted_iota(jnp.int32, shape, 0)
        col_i = lax.broadcasted_iota(jnp.int32, shape, 1)
        row = row_i.astype(F32)
        diff = (row_i - col_i).astype(F32)
        for h in range(N_RET_HEADS):
            d_fwd = jnp.where(diff >= 0, jnp.exp(ldf_ref[h] * jnp.maximum(diff, 0.0)), 0.0)
            d_bwd = jnp.where(diff <= 0, jnp.exp(ldb_ref[h] * jnp.maximum(-diff, 0.0)), 0.0)
            tab_ref[MASK, h] = d_fwd + d_bwd
            tab_ref[Q_FWD, h] = jnp.exp(ldf_ref[h] * (row + 1.0))
            tab_ref[Q_BWD, h] = jnp.exp(ldb_ref[h] * (float(RET_CHUNK) - row))

    yield

    units = [(c, h) for c in range(nchunk) for h in range(N_RET_HEADS)]

    for u, (c, h) in enumerate(units):
        q = rq_ref[0, _chunk_rows(c), _head_cols(h)]
        k = rk_ref[0, _chunk_rows(c), _head_cols(h)]
        inner_ref[u] = (_dot_nt(q, k) * tab_ref[MASK, h]).astype(BF16)
        kv_ref[c, :, _head_cols(h)] = _dot_tn(rkf_ref[0, _chunk_rows(c), _head_cols(h)],
                                              rv_ref[0, _chunk_rows(c), _head_cols(h)])

    yield

    chunk_decay = _chunk_decay(ldf_ref)
    state = state_ref[...]
    for c in range(nchunk):
        fwd = state.astype(BF16)
        for h in range(N_RET_HEADS):
            base = 2 * h * RET_HEAD_DIM
            cs_ref[c, :, base:base + RET_HEAD_DIM] = fwd[:, _head_cols(h)]
            cs_ref[c, :, base + RET_HEAD_DIM:base + 2 * RET_HEAD_DIM] = sb_ref[0, c, :, _head_cols(h)]
        state = state * chunk_decay + kv_ref[c]
    state_ref[...] = state

    yield

    for u, (c, h) in enumerate(units):
        q = rq_ref[0, _chunk_rows(c), _head_cols(h)]
        o = _dot(inner_ref[u], rv_ref[0, _chunk_rows(c), _head_cols(h)])
        cross = _dot(q, cs_ref[c, :, 2 * h * RET_HEAD_DIM:2 * (h + 1) * RET_HEAD_DIM])
        o += tab_ref[Q_FWD, h] * cross[:, :RET_HEAD_DIM] + tab_ref[Q_BWD, h] * cross[:, RET_HEAD_DIM:]
        mu = jnp.mean(o, axis=-1, keepdims=True)
        var = jnp.mean(jnp.square(o - mu), axis=-1, keepdims=True)
        on = (o - mu) * lax.rsqrt(var + EPS)
        gate = gate_ref[0, _chunk_rows(c), _head_cols(h)].astype(F32)
        o_ref[0, _chunk_rows(c), _head_cols(h)] = (gate * on).astype(BF16)


def _ret_out_specs(b, s, ts):
    cpt = ts // RET_CHUNK
    seq_spec = pl.BlockSpec((1, ts, RET_WIDTH), lambda i, t: (i, t, 0))
    in_specs = [pl.BlockSpec(memory_space=pltpu.SMEM), pl.BlockSpec(memory_space=pltpu.SMEM),
                seq_spec, seq_spec, seq_spec, seq_spec, seq_spec,
                pl.BlockSpec((1, cpt, RET_HEAD_DIM, RET_WIDTH), lambda i, t: (i, t, 0, 0))]
    scratch = [pltpu.VMEM((RET_HEAD_DIM, RET_WIDTH), F32),
               pltpu.VMEM((3, N_RET_HEADS, RET_CHUNK, RET_HEAD_DIM), F32),
               pltpu.VMEM((cpt * N_RET_HEADS, RET_CHUNK, RET_CHUNK), BF16),
               pltpu.VMEM((cpt, RET_HEAD_DIM, RET_WIDTH), F32),
               pltpu.VMEM((cpt, RET_HEAD_DIM, 2 * RET_WIDTH), BF16)]
    return in_specs, seq_spec, jax.ShapeDtypeStruct((b, s, RET_WIDTH), BF16), scratch


def _mix(attn_args, ret_args):
    sink, q, k, v, gain = attn_args
    b, s, _ = q.shape
    ts = min(SEQ_TILE, s)
    a_in, a_out, a_shape, a_scratch = _attn_specs(b, s, ts)
    r_in, r_out, r_shape, r_scratch = _ret_out_specs(b, s, ts)
    n_a_in, n_r_in, n_a_scr = len(a_in), len(r_in), len(a_scratch)

    def mix_kernel(*refs):
        ins, (attn_o, ret_o), scr = refs[:n_a_in + n_r_in], refs[n_a_in + n_r_in:][:2], refs[n_a_in + n_r_in + 2:]
        ret = _ret_out_kernel(*ins[n_a_in:], ret_o, *scr[n_a_scr:])
        attn = _attn_kernel(*ins[:n_a_in], attn_o, *scr[:n_a_scr])
        for stage in (attn, ret, attn, ret, ret, attn, ret, attn):
            next(stage, None)

    return pl.pallas_call(
        mix_kernel,
        grid=(b, s // ts),
        in_specs=a_in + r_in,
        out_specs=[a_out, r_out],
        out_shape=[a_shape, r_shape],
        scratch_shapes=a_scratch + r_scratch,
        compiler_params=pltpu.CompilerParams(dimension_semantics=("parallel", "arbitrary"),
                                             vmem_limit_bytes=VMEM_LIMIT),
        name="mix",
    )(sink, q, k, k, k, v, v, v, gain, *ret_args)


def _out_ffn_kernel(h_ref, a_ref, r_ref, woa_ref, wor_ref, g2_ref, wg_ref, wu_ref, wd_ref, gf_ref,
                    y_ref, act_ref):
    groups = _row_groups(h_ref.shape[0])
    hs = [h_ref[r, :] + _dot(a_ref[r, :], woa_ref[...]) + _dot(r_ref[r, :], wor_ref[...]) for r in groups]
    hns = [_rmsnorm(h, g2_ref[...]).astype(BF16) for h in hs]
    for r, hn in zip(groups, hns):
        _gate_up(hn, r, wg_ref, wu_ref, act_ref)
    ys = [_dot(act_ref[r, :], wd_ref[...]) for r in groups]
    for r, h, y in zip(groups, hs, ys):
        y_ref[r, :] = _rmsnorm(h + 0.5 * y, gf_ref[...])


def _out_ffn(h, a, r, woa, wor, g2, wg, wu, wd, gf):
    n = h.shape[0]
    tm = min(OUT_TOKEN_TILE, n)
    row = lambda i: (i, 0)
    return pl.pallas_call(
        _out_ffn_kernel,
        grid=(n // tm,),
        in_specs=[
            pl.BlockSpec((tm, D_MODEL), row),
            pl.BlockSpec((tm, ATTN_WIDTH), row),
            pl.BlockSpec((tm, RET_WIDTH), row),
            _const_spec((ATTN_WIDTH, D_MODEL)),
            _const_spec((RET_WIDTH, D_MODEL)),
            _const_spec((1, D_MODEL)),
            _const_spec((D_MODEL, D_FF)),
            _const_spec((D_MODEL, D_FF)),
            _const_spec((D_FF, D_MODEL)),
            _const_spec((1, D_MODEL)),
        ],
        out_specs=pl.BlockSpec((tm, D_MODEL), row),
        out_shape=jax.ShapeDtypeStruct((n, D_MODEL), F32),
        scratch_shapes=[pltpu.VMEM((tm, D_FF), BF16)],
        compiler_params=pltpu.CompilerParams(dimension_semantics=("parallel",),
                                             vmem_limit_bytes=VMEM_LIMIT),
        name="out_ffn",
    )(h, a, r, woa, wor, g2, wg, wu, wd, gf)


def _rope_tables(seq_len):
    half = ROT_DIM // 2
    lane = jnp.arange(LANES) % ATTN_HEAD_DIM
    inv_a = ROPE_THETA ** (-jnp.arange(0, ROT_DIM, 2, dtype=F32) / ROT_DIM)
    freq_a = jnp.where(lane < ROT_DIM, inv_a[lane % half], 0.0)
    inv_r = RET_ROT_THETA ** (-jnp.linspace(0.0, 1.0, RET_HEAD_DIM // 2, dtype=F32))
    freq_r = jnp.concatenate([inv_r, inv_r])

    start = (jnp.arange(seq_len // ROPE_BLOCK, dtype=F32) * ROPE_BLOCK)[:, None, None]
    offset = jnp.arange(ROPE_BLOCK, dtype=F32)[None, :, None]

    def cos_sin(freq):
        a, b = start * freq, offset * freq
        cos = jnp.cos(a) * jnp.cos(b) - jnp.sin(a) * jnp.sin(b)
        sin = jnp.sin(a) * jnp.cos(b) + jnp.cos(a) * jnp.sin(b)
        return cos.reshape(seq_len, LANES), sin.reshape(seq_len, LANES)

    cos_a, sin = cos_sin(freq_a)
    sin_a = sin * jnp.where(lane < half, -1.0, 0.0)[None, :]
    sin_b = sin * jnp.where((lane >= half) & (lane < ROT_DIM), 1.0, 0.0)[None, :]
    cos_r, sin = cos_sin(freq_r)
    sign_r = jnp.where(jnp.arange(LANES) < RET_HEAD_DIM // 2, -1.0, 1.0)
    return jnp.concatenate([cos_a, sin_a, sin_b, cos_r, sin * sign_r[None, :]], axis=1)


def _layer(x, p, tables):
    b, s, _ = x.shape
    n = b * s
    h, un = _ffn(x.reshape(n, D_MODEL), p["g1"], p["wg1"], p["wu1"], p["wd1"], p["gm"])
    q, k, v, rq, rk, rkf, rv, gate, sb = _proj(un, s, p["win"], tables, p["ldf"], p["ldb"])
    seq = lambda a: a.reshape(b, s, a.shape[-1])
    sb = sb.reshape(b, s // RET_CHUNK, RET_HEAD_DIM, RET_WIDTH)
    attn, ret = _mix((p["sink"], seq(q), seq(k), seq(v), p["ga"]),
                     (p["ldf"], p["ldb"], seq(rq), seq(rk), seq(rkf), seq(rv), seq(gate), sb))
    y = _out_ffn(h, attn.reshape(n, ATTN_WIDTH), ret.reshape(n, RET_WIDTH),
                 p["woa"], p["wor"], p["g2"], p["wg2"], p["wu2"], p["wd2"], p["gf"])
    return y.reshape(b, s, D_MODEL)


def kernel(x_prompt, x_sample, ffn1_norm, ffn1_w_gate, ffn1_w_up, ffn1_w_down, mix_norm, w_in, attn_sink, attn_out_norm, ret_log_decay_fwd, ret_log_decay_bwd, w_out, ffn2_norm, ffn2_w_gate, ffn2_w_up, ffn2_w_down, final_norm):
    assert ffn1_norm.shape[0] == 1
    wo = w_out[0].astype(BF16)
    p = dict(
        g1=ffn1_norm[0][None, :], wg1=ffn1_w_gate[0].astype(BF16), wu1=ffn1_w_up[0].astype(BF16),
        wd1=ffn1_w_down[0].astype(BF16), gm=mix_norm[0][None, :],
        win=w_in[0].astype(BF16),
        sink=attn_sink[0], ga=attn_out_norm[0][None, :],
        ldf=ret_log_decay_fwd[0], ldb=ret_log_decay_bwd[0],
        woa=wo[:ATTN_WIDTH], wor=wo[ATTN_WIDTH:],
        g2=ffn2_norm[0][None, :], wg2=ffn2_w_gate[0].astype(BF16), wu2=ffn2_w_up[0].astype(BF16),
        wd2=ffn2_w_down[0].astype(BF16), gf=final_norm[None, :],
    )
    tables = _rope_tables(max(x_prompt.shape[1], x_sample.shape[1]))
    return (_layer(x_prompt, p, tables), _layer(x_sample, p, tables))
```

```python
import functools

import jax
import jax.numpy as jnp
from jax import lax
from jax.experimental import pallas as pl
from jax.experimental.pallas import tpu as pltpu

D_MODEL = 1024
D_FF = 2816
N_ATTN_HEADS = 8
N_KV_HEADS = 2
ATTN_HEAD_DIM = 64
ATTN_WIDTH = N_ATTN_HEADS * ATTN_HEAD_DIM
WINDOW = 128
ATTN_BLOCK = 128
ROT_DIM = 16
ROPE_THETA = 500000.0
RET_HEAD_DIM = 128
N_RET_HEADS = 4
RET_WIDTH = N_RET_HEADS * RET_HEAD_DIM
RET_CHUNK = 128
RET_ROT_THETA = 10000.0
EPS = 1e-6
NEG_BIG = -1e30
LOG2_E = 1.4426950408889634
ATTN_Q_SCALE = ATTN_HEAD_DIM ** -0.5 * LOG2_E

LANES = 128
KV_DUP_WIDTH = 2 * LANES
FF_CHUNK = 256
PROJ_CHUNK = 512
ROW_GROUP = 256
FFN_TOKEN_TILE = 1024
PROJ_TOKEN_TILE = 1024
OUT_TOKEN_TILE = 1024
SEQ_TILE = 1024
ROPE_BLOCK = 128
VMEM_LIMIT = 56 * 1024 * 1024

BF16 = jnp.bfloat16
F32 = jnp.float32


def _dot(a, b):
    return jnp.dot(a, b, preferred_element_type=F32)


def _dot_nt(a, b):
    return lax.dot_general(a, b, (((1,), (1,)), ((), ())), preferred_element_type=F32)


def _dot_tn(a, b):
    return lax.dot_general(a, b, (((0,), (0,)), ((), ())), preferred_element_type=F32)


def _rmsnorm(x, g):
    return x * lax.rsqrt(jnp.mean(x * x, axis=-1, keepdims=True) + EPS) * g


def _silu(x):
    return x * (1.0 / (1.0 + jnp.exp(-x)))


def _row_groups(rows):
    return [slice(i, i + ROW_GROUP) for i in range(0, rows, ROW_GROUP)]


def _gate_up(xn, rows, wg_ref, wu_ref, act_ref):
    for c in range(D_FF // FF_CHUNK):
        cols = slice(c * FF_CHUNK, (c + 1) * FF_CHUNK)
        g = _dot(xn, wg_ref[:, cols])
        u = _dot(xn, wu_ref[:, cols])
        act_ref[rows, cols] = (_silu(g) * u).astype(BF16)


def _ffn_kernel(x_ref, g1_ref, wg_ref, wu_ref, wd_ref, gm_ref, h_ref, un_ref, act_ref):
    groups = _row_groups(x_ref.shape[0])
    xns = [_rmsnorm(x_ref[r, :], g1_ref[...]).astype(BF16) for r in groups]
    for r, xn in zip(groups, xns):
        _gate_up(xn, r, wg_ref, wu_ref, act_ref)
    ys = [_dot(act_ref[r, :], wd_ref[...]) for r in groups]
    for r, y in zip(groups, ys):
        h = x_ref[r, :] + 0.5 * y
        h_ref[r, :] = h
        un_ref[r, :] = _rmsnorm(h, gm_ref[...]).astype(BF16)


def _proj_kernel(tiles_per_seq, ldf_ref, ldb_ref, un_ref, win_ref, rope_ref,
                 q_ref, k_ref, v_ref, rq_ref, rk_ref, rkf_ref, rv_ref, gate_ref, sb_ref,
                 decay_ref, rkb_ref, state_ref, kv_ref):
    @pl.when(pl.program_id(0) == 0)
    def _():
        shape = decay_ref.shape[1:]
        j = (lax.broadcasted_iota(jnp.int32, shape, 0) % RET_CHUNK).astype(F32)
        decay_ref[0] = jnp.exp(_head_lane_values(ldf_ref) * (float(RET_CHUNK) - 1.0 - j))
        decay_ref[1] = jnp.exp(_head_lane_values(ldb_ref) * j)

    @pl.when(pl.program_id(0) % tiles_per_seq == 0)
    def _():
        state_ref[...] = jnp.zeros_like(state_ref)

    un = un_ref[...]
    cosa, sina, sinb, cosr, sinr = (rope_ref[:, t * LANES:(t + 1) * LANES] for t in range(5))

    def rope_a(xb):
        return (xb * cosa + pltpu.roll(xb, LANES - ROT_DIM // 2, 1) * sina
                + pltpu.roll(xb, ROT_DIM // 2, 1) * sinb)

    def rope_r(xb):
        return xb * cosr + pltpu.roll(xb, RET_HEAD_DIM // 2, 1) * sinr

    def lane_blocks(p):
        return [p[:, c * LANES:(c + 1) * LANES] for c in range(p.shape[1] // LANES)]

    def proj(section):
        kv_cols = 2 * N_KV_HEADS * ATTN_HEAD_DIM
        if section == 0:
            cols = slice(0, ATTN_WIDTH)
        elif section == 1:
            cols = slice(ATTN_WIDTH, ATTN_WIDTH + kv_cols)
        else:
            start = ATTN_WIDTH + kv_cols + (section - 2) * PROJ_CHUNK
            cols = slice(start, start + PROJ_CHUNK)
        return _dot(un, win_ref[:, cols])

    for c, blk in enumerate(lane_blocks(proj(3))):
        cols = slice(c * LANES, (c + 1) * LANES)
        rk = rope_r(blk) * (RET_HEAD_DIM ** -0.5)
        rk_ref[:, cols] = rk.astype(BF16)
        rkf_ref[:, cols] = (rk * decay_ref[0, :, cols]).astype(BF16)
        rkb_ref[:, cols] = (rk * decay_ref[1, :, cols]).astype(BF16)
    rv_ref[...] = proj(4).astype(BF16)
    gate_ref[...] = _silu(proj(5)).astype(BF16)
    for c, blk in enumerate(lane_blocks(proj(0))):
        q_ref[:, c * LANES:(c + 1) * LANES] = (rope_a(blk) * ATTN_Q_SCALE).astype(BF16)
    for c, blk in enumerate(lane_blocks(proj(2))):
        rq_ref[:, c * LANES:(c + 1) * LANES] = rope_r(blk).astype(BF16)

    nchunk = un_ref.shape[0] // RET_CHUNK
    for c in range(nchunk):
        for h in range(N_RET_HEADS):
            kv_ref[c, :, _head_cols(h)] = _dot_tn(rkb_ref[_chunk_rows(c), _head_cols(h)],
                                                  rv_ref[_chunk_rows(c), _head_cols(h)])
    chunk_decay = _chunk_decay(ldb_ref)
    state = state_ref[...]
    for c in reversed(range(nchunk)):
        sb_ref[c] = state.astype(BF16)
        state = state * chunk_decay + kv_ref[c]
    state_ref[...] = state

    kblk, vblk = lane_blocks(proj(1))
    lo_lanes = lax.broadcasted_iota(jnp.int32, kblk.shape, 1) < ATTN_HEAD_DIM
    for ref, blk in ((k_ref, rope_a(kblk)), (v_ref, vblk)):
        swapped = pltpu.roll(blk, ATTN_HEAD_DIM, 1)
        ref[:, 0:LANES] = jnp.where(lo_lanes, blk, swapped).astype(BF16)
        ref[:, LANES:2 * LANES] = jnp.where(lo_lanes, swapped, blk).astype(BF16)


def _const_spec(shape):
    return pl.BlockSpec(shape, lambda *_: (0,) * len(shape), pipeline_mode=pl.Buffered(1))


def _ffn(x2d, g1, wg, wu, wd, gm):
    n = x2d.shape[0]
    tm = min(FFN_TOKEN_TILE, n)
    row = lambda i: (i, 0)
    return pl.pallas_call(
        _ffn_kernel,
        grid=(n // tm,),
        in_specs=[
            pl.BlockSpec((tm, D_MODEL), row),
            _const_spec((1, D_MODEL)),
            _const_spec((D_MODEL, D_FF)),
            _const_spec((D_MODEL, D_FF)),
            _const_spec((D_FF, D_MODEL)),
            _const_spec((1, D_MODEL)),
        ],
        out_specs=[pl.BlockSpec((tm, D_MODEL), row), pl.BlockSpec((tm, D_MODEL), row)],
        out_shape=[jax.ShapeDtypeStruct((n, D_MODEL), F32), jax.ShapeDtypeStruct((n, D_MODEL), BF16)],
        scratch_shapes=[pltpu.VMEM((tm, D_FF), BF16)],
        compiler_params=pltpu.CompilerParams(dimension_semantics=("parallel",),
                                             vmem_limit_bytes=VMEM_LIMIT),
        name="ffn",
    )(x2d, g1, wg, wu, wd, gm)


def _proj(un, seq_len, win, tables, ld_f, ld_b):
    n = un.shape[0]
    tm = min(PROJ_TOKEN_TILE, seq_len)
    assert tm % RET_CHUNK == 0
    tiles_per_seq = seq_len // tm
    nt = n // tm
    cpt = tm // RET_CHUNK
    row = lambda i: (nt - 1 - i, 0)
    tab = lambda i: ((nt - 1 - i) % tiles_per_seq, 0)
    out_widths = (ATTN_WIDTH, KV_DUP_WIDTH, KV_DUP_WIDTH) + (RET_WIDTH,) * 5
    return pl.pallas_call(
        functools.partial(_proj_kernel, tiles_per_seq),
        grid=(nt,),
        in_specs=[
            pl.BlockSpec(memory_space=pltpu.SMEM),
            pl.BlockSpec(memory_space=pltpu.SMEM),
            pl.BlockSpec((tm, D_MODEL), row),
            _const_spec(win.shape),
            pl.BlockSpec((tm, tables.shape[1]), tab),
        ],
        out_specs=[pl.BlockSpec((tm, w), row) for w in out_widths]
                  + [pl.BlockSpec((cpt, RET_HEAD_DIM, RET_WIDTH), lambda i: (nt - 1 - i, 0, 0))],
        out_shape=[jax.ShapeDtypeStruct((n, w), BF16) for w in out_widths]
                  + [jax.ShapeDtypeStruct((n // RET_CHUNK, RET_HEAD_DIM, RET_WIDTH), BF16)],
        scratch_shapes=[pltpu.VMEM((2, tm, RET_WIDTH), F32),
                        pltpu.VMEM((tm, RET_WIDTH), BF16),
                        pltpu.VMEM((RET_HEAD_DIM, RET_WIDTH), F32),
                        pltpu.VMEM((cpt, RET_HEAD_DIM, RET_WIDTH), F32)],
        compiler_params=pltpu.CompilerParams(dimension_semantics=("arbitrary",),
                                             vmem_limit_bytes=VMEM_LIMIT),
        name="proj",
    )(ld_f, ld_b, un, win, tables)


def _attn_kernel(sink_ref, q_ref, kp_ref, km_ref, kn_ref, vp_ref, vm_ref, vn_ref, gain_ref,
                 o_ref, klo_ref, khi_ref, vlo_ref, vhi_ref, bias_ref, s_ref, p_ref, e_ref, acc_ref):
    t = pl.program_id(1)
    nt = pl.num_programs(1)
    ts = q_ref.shape[1]
    nblk = ts // ATTN_BLOCK
    wk = 3 * ATTN_BLOCK

    @pl.when(t == 0)
    def _():
        qi = lax.broadcasted_iota(jnp.int32, (ATTN_BLOCK, wk), 0)
        kj = lax.broadcasted_iota(jnp.int32, (ATTN_BLOCK, wk), 1)
        band = (kj >= qi) & (kj <= qi + 2 * WINDOW)
        bias_ref[0] = jnp.where(band, 0.0, NEG_BIG)
        bias_ref[1] = jnp.where(band & (kj >= ATTN_BLOCK), 0.0, NEG_BIG)
        bias_ref[2] = jnp.where(band & (kj < 2 * ATTN_BLOCK), 0.0, NEG_BIG)

    yield

    lo_lanes = (lax.broadcasted_iota(jnp.int32, (1, KV_DUP_WIDTH), 1) % LANES) < ATTN_HEAD_DIM
    window = ((slice(0, ATTN_BLOCK), kp_ref, vp_ref), (slice(ATTN_BLOCK, ATTN_BLOCK + ts), km_ref, vm_ref),
              (slice(ATTN_BLOCK + ts, 2 * ATTN_BLOCK + ts), kn_ref, vn_ref))
    for dst, k_src, v_src in window:
        kblk, vblk = k_src[0], v_src[0]
        klo_ref[dst, :] = jnp.where(lo_lanes, kblk, jnp.zeros_like(kblk))
        khi_ref[dst, :] = jnp.where(lo_lanes, jnp.zeros_like(kblk), kblk)
        vlo_ref[dst, :] = jnp.where(lo_lanes, vblk, jnp.ones_like(vblk))
        vhi_ref[dst, :] = jnp.where(lo_lanes, jnp.ones_like(vblk), vblk)

    out_lo = lax.broadcasted_iota(jnp.int32, (ATTN_BLOCK, LANES), 1) < ATTN_HEAD_DIM
    first_idx = jnp.where(t == 0, 1, 0)
    last_idx = jnp.where(t == nt - 1, 2, 0)

    units = [(j, g, pair, half) for j in range(nblk) for g in range(N_KV_HEADS)
             for pair in range(2) for half in range(2)]

    def rows_of(j):
        return slice(j * ATTN_BLOCK, (j + 1) * ATTN_BLOCK)

    def win_rows(j):
        return slice(j * ATTN_BLOCK, j * ATTN_BLOCK + wk)

    def pair_cols(g, pair):
        return slice((2 * g + pair) * LANES, (2 * g + pair + 1) * LANES)

    for u, (j, g, pair, half) in enumerate(units):
        k_ref = khi_ref if half else klo_ref
        bias_idx = first_idx if j == 0 else (last_idx if j == nblk - 1 else 0)
        s = _dot_nt(q_ref[0, rows_of(j), pair_cols(g, pair)],
                    k_ref[win_rows(j), g * LANES:(g + 1) * LANES])
        for blk in range(3):
            cols = slice(blk * ATTN_BLOCK, (blk + 1) * ATTN_BLOCK)
            s_ref[u, :, cols] = s[:, cols] if blk == 1 else s[:, cols] + bias_ref[bias_idx, :, cols]

    yield

    for u, (j, g, pair, half) in enumerate(units):
        s = s_ref[u]
        sink = sink_ref[4 * g + 2 * pair + half] * LOG2_E
        m = jnp.maximum(jnp.max(s, axis=-1, keepdims=True), sink)
        p_ref[u] = jnp.exp2(s - m).astype(BF16)
        e_ref[u] = jnp.broadcast_to(jnp.exp2(sink - m), (ATTN_BLOCK, LANES))

    yield

    for u in range(0, len(units), 2):
        j, g, pair, _ = units[u]
        kv_cols = slice(g * LANES, (g + 1) * LANES)
        even = _dot(p_ref[u], vlo_ref[win_rows(j), kv_cols])
        odd = _dot(p_ref[u + 1], vhi_ref[win_rows(j), kv_cols])
        numer = jnp.where(out_lo, even, odd)
        rowsum = pltpu.roll(jnp.where(out_lo, odd, even), ATTN_HEAD_DIM, 1)
        denom = rowsum + jnp.where(out_lo, e_ref[u], e_ref[u + 1])
        acc_ref[rows_of(j), pair_cols(g, pair)] = numer * (1.0 / denom)

    o_ref[0] = _rmsnorm(acc_ref[...], gain_ref[...]).astype(BF16)


def _attn_specs(b, s, ts):
    bps = ts // ATTN_BLOCK
    nb = s // ATTN_BLOCK
    assert bps >= 2
    main = lambda i, t: (i, t, 0)
    prev = lambda i, t: (i, jnp.maximum(t * bps - 1, 0), 0)
    nxt = lambda i, t: (i, jnp.minimum((t + 1) * bps, nb - 1), 0)
    kv_specs = [pl.BlockSpec((1, ATTN_BLOCK, KV_DUP_WIDTH), prev),
                pl.BlockSpec((1, ts, KV_DUP_WIDTH), main),
                pl.BlockSpec((1, ATTN_BLOCK, KV_DUP_WIDTH), nxt)]
    in_specs = ([pl.BlockSpec(memory_space=pltpu.SMEM), pl.BlockSpec((1, ts, ATTN_WIDTH), main)]
                + kv_specs + kv_specs + [pl.BlockSpec((1, ATTN_WIDTH), lambda i, t: (0, 0))])
    scratch = [pltpu.VMEM((ts + 2 * ATTN_BLOCK, KV_DUP_WIDTH), BF16)] * 4 + [
        pltpu.VMEM((3, ATTN_BLOCK, 3 * ATTN_BLOCK), F32),
        pltpu.VMEM((bps * N_ATTN_HEADS, ATTN_BLOCK, 3 * ATTN_BLOCK), F32),
        pltpu.VMEM((bps * N_ATTN_HEADS, ATTN_BLOCK, 3 * ATTN_BLOCK), BF16),
        pltpu.VMEM((bps * N_ATTN_HEADS, ATTN_BLOCK, LANES), F32),
        pltpu.VMEM((ts, ATTN_WIDTH), F32)]
    return (in_specs, pl.BlockSpec((1, ts, ATTN_WIDTH), main),
            jax.ShapeDtypeStruct((b, s, ATTN_WIDTH), BF16), scratch)


def _head_lane_values(ld_ref):
    lane = lax.broadcasted_iota(jnp.int32, (1, RET_WIDTH), 1)
    ld = jnp.zeros((1, RET_WIDTH), F32)
    for h in range(N_RET_HEADS):
        ld = jnp.where(lane >= h * RET_HEAD_DIM, ld_ref[h], ld)
    return ld


def _chunk_decay(ld_ref):
    return jnp.exp(_head_lane_values(ld_ref) * float(RET_CHUNK))


def _head_cols(h):
    return slice(h * RET_HEAD_DIM, (h + 1) * RET_HEAD_DIM)


def _chunk_rows(c):
    return slice(c * RET_CHUNK, (c + 1) * RET_CHUNK)


def _ret_out_kernel(ldf_ref, ldb_ref, rq_ref, rk_ref, rkf_ref, rv_ref, gate_ref, sb_ref, o_ref,
                    state_ref, tab_ref, inner_ref, kv_ref, cs_ref):
    t = pl.program_id(1)
    nchunk = rq_ref.shape[1] // RET_CHUNK
    MASK, Q_FWD, Q_BWD = range(3)

    @pl.when(t == 0)
    def _():
        state_ref[...] = jnp.zeros_like(state_ref)
        shape = (RET_CHUNK, RET_HEAD_DIM)
        row_i = lax.broadcasted_iota(jnp.int32, shape, 0)
        col_i = lax.broadcasted_iota(jnp.int32, shape, 1)
        row = row_i.astype(F32)
        diff = (row_i - col_i).astype(F32)
        for h in range(N_RET_HEADS):
            d_fwd = jnp.where(diff >= 0, jnp.exp(ldf_ref[h] * jnp.maximum(diff, 0.0)), 0.0)
            d_bwd = jnp.where(diff <= 0, jnp.exp(ldb_ref[h] * jnp.maximum(-diff, 0.0)), 0.0)
            tab_ref[MASK, h] = d_fwd + d_bwd
            tab_ref[Q_FWD, h] = jnp.exp(ldf_ref[h] * (row + 1.0))
            tab_ref[Q_BWD, h] = jnp.exp(ldb_ref[h] * (float(RET_CHUNK) - row))

    yield

    units = [(c, h) for c in range(nchunk) for h in range(N_RET_HEADS)]

    for u, (c, h) in enumerate(units):
        q = rq_ref[0, _chunk_rows(c), _head_cols(h)]
        k = rk_ref[0, _chunk_rows(c), _head_cols(h)]
        inner_ref[u] = (_dot_nt(q, k) * tab_ref[MASK, h]).astype(BF16)
        kv_ref[c, :, _head_cols(h)] = _dot_tn(rkf_ref[0, _chunk_rows(c), _head_cols(h)],
                                              rv_ref[0, _chunk_rows(c), _head_cols(h)])

    yield

    chunk_decay = _chunk_decay(ldf_ref)
    state = state_ref[...]
    for c in range(nchunk):
        fwd = state.astype(BF16)
        for h in range(N_RET_HEADS):
            base = 2 * h * RET_HEAD_DIM
            cs_ref[c, :, base:base + RET_HEAD_DIM] = fwd[:, _head_cols(h)]
            cs_ref[c, :, base + RET_HEAD_DIM:base + 2 * RET_HEAD_DIM] = sb_ref[0, c, :, _head_cols(h)]
        state = state * chunk_decay + kv_ref[c]
    state_ref[...] = state

    yield

    for u, (c, h) in enumerate(units):
        q = rq_ref[0, _chunk_rows(c), _head_cols(h)]
        o = _dot(inner_ref[u], rv_ref[0, _chunk_rows(c), _head_cols(h)])
        cross = _dot(q, cs_ref[c, :, 2 * h * RET_HEAD_DIM:2 * (h + 1) * RET_HEAD_DIM])
        o += tab_ref[Q_FWD, h] * cross[:, :RET_HEAD_DIM] + tab_ref[Q_BWD, h] * cross[:, RET_HEAD_DIM:]
        mu = jnp.mean(o, axis=-1, keepdims=True)
        var = jnp.mean(jnp.square(o - mu), axis=-1, keepdims=True)
        on = (o - mu) * lax.rsqrt(var + EPS)
        gate = gate_ref[0, _chunk_rows(c), _head_cols(h)].astype(F32)
        o_ref[0, _chunk_rows(c), _head_cols(h)] = (gate * on).astype(BF16)


def _ret_out_specs(b, s, ts):
    cpt = ts // RET_CHUNK
    seq_spec = pl.BlockSpec((1, ts, RET_WIDTH), lambda i, t: (i, t, 0))
    in_specs = [pl.BlockSpec(memory_space=pltpu.SMEM), pl.BlockSpec(memory_space=pltpu.SMEM),
                seq_spec, seq_spec, seq_spec, seq_spec, seq_spec,
                pl.BlockSpec((1, cpt, RET_HEAD_DIM, RET_WIDTH), lambda i, t: (i, t, 0, 0))]
    scratch = [pltpu.VMEM((RET_HEAD_DIM, RET_WIDTH), F32),
               pltpu.VMEM((3, N_RET_HEADS, RET_CHUNK, RET_HEAD_DIM), F32),
               pltpu.VMEM((cpt * N_RET_HEADS, RET_CHUNK, RET_CHUNK), BF16),
               pltpu.VMEM((cpt, RET_HEAD_DIM, RET_WIDTH), F32),
               pltpu.VMEM((cpt, RET_HEAD_DIM, 2 * RET_WIDTH), BF16)]
    return in_specs, seq_spec, jax.ShapeDtypeStruct((b, s, RET_WIDTH), BF16), scratch


def _mix(attn_args, ret_args):
    sink, q, k, v, gain = attn_args
    b, s, _ = q.shape
    ts = min(SEQ_TILE, s)
    a_in, a_out, a_shape, a_scratch = _attn_specs(b, s, ts)
    r_in, r_out, r_shape, r_scratch = _ret_out_specs(b, s, ts)
    n_a_in, n_r_in, n_a_scr = len(a_in), len(r_in), len(a_scratch)

    def mix_kernel(*refs):
        ins, (attn_o, ret_o), scr = refs[:n_a_in + n_r_in], refs[n_a_in + n_r_in:][:2], refs[n_a_in + n_r_in + 2:]
        ret = _ret_out_kernel(*ins[n_a_in:], ret_o, *scr[n_a_scr:])
        attn = _attn_kernel(*ins[:n_a_in], attn_o, *scr[:n_a_scr])
        for stage in (attn, ret, attn, ret, ret, attn, ret, attn):
            next(stage, None)

    return pl.pallas_call(
        mix_kernel,
        grid=(b, s // ts),
        in_specs=a_in + r_in,
        out_specs=[a_out, r_out],
        out_shape=[a_shape, r_shape],
        scratch_shapes=a_scratch + r_scratch,
        compiler_params=pltpu.CompilerParams(dimension_semantics=("parallel", "arbitrary"),
                                             vmem_limit_bytes=VMEM_LIMIT),
        name="mix",
    )(sink, q, k, k, k, v, v, v, gain, *ret_args)


def _out_ffn_kernel(h_ref, a_ref, r_ref, woa_ref, wor_ref, g2_ref, wg_ref, wu_ref, wd_ref, gf_ref,
                    y_ref, act_ref):
    groups = _row_groups(h_ref.shape[0])
    hs = [h_ref[r, :] + _dot(a_ref[r, :], woa_ref[...]) + _dot(r_ref[r, :], wor_ref[...]) for r in groups]
    hns = [_rmsnorm(h, g2_ref[...]).astype(BF16) for h in hs]
    for r, hn in zip(groups, hns):
        _gate_up(hn, r, wg_ref, wu_ref, act_ref)
    ys = [_dot(act_ref[r, :], wd_ref[...]) for r in groups]
    for r, h, y in zip(groups, hs, ys):
        y_ref[r, :] = _rmsnorm(h + 0.5 * y, gf_ref[...])


def _out_ffn(h, a, r, woa, wor, g2, wg, wu, wd, gf):
    n = h.shape[0]
    tm = min(OUT_TOKEN_TILE, n)
    row = lambda i: (i, 0)
    return pl.pallas_call(
        _out_ffn_kernel,
        grid=(n // tm,),
        in_specs=[
            pl.BlockSpec((tm, D_MODEL), row),
            pl.BlockSpec((tm, ATTN_WIDTH), row),
            pl.BlockSpec((tm, RET_WIDTH), row),
            _const_spec((ATTN_WIDTH, D_MODEL)),
            _const_spec((RET_WIDTH, D_MODEL)),
            _const_spec((1, D_MODEL)),
            _const_spec((D_MODEL, D_FF)),
            _const_spec((D_MODEL, D_FF)),
            _const_spec((D_FF, D_MODEL)),
            _const_spec((1, D_MODEL)),
        ],
        out_specs=pl.BlockSpec((tm, D_MODEL), row),
        out_shape=jax.ShapeDtypeStruct((n, D_MODEL), F32),
        scratch_shapes=[pltpu.VMEM((tm, D_FF), BF16)],
        compiler_params=pltpu.CompilerParams(dimension_semantics=("parallel",),
                                             vmem_limit_bytes=VMEM_LIMIT),
        name="out_ffn",
    )(h, a, r, woa, wor, g2, wg, wu, wd, gf)


def _rope_tables(seq_len):
    half = ROT_DIM // 2
    lane = jnp.arange(LANES) % ATTN_HEAD_DIM
    inv_a = ROPE_THETA ** (-jnp.arange(0, ROT_DIM, 2, dtype=F32) / ROT_DIM)
    freq_a = jnp.where(lane < ROT_DIM, inv_a[lane % half], 0.0)
    inv_r = RET_ROT_THETA ** (-jnp.linspace(0.0, 1.0, RET_HEAD_DIM // 2, dtype=F32))
    freq_r = jnp.concatenate([inv_r, inv_r])

    start = (jnp.arange(seq_len // ROPE_BLOCK, dtype=F32) * ROPE_BLOCK)[:, None, None]
    offset = jnp.arange(ROPE_BLOCK, dtype=F32)[None, :, None]

    def cos_sin(freq):
        a, b = start * freq, offset * freq
        cos = jnp.cos(a) * jnp.cos(b) - jnp.sin(a) * jnp.sin(b)
        sin = jnp.sin(a) * jnp.cos(b) + jnp.cos(a) * jnp.sin(b)
        return cos.reshape(seq_len, LANES), sin.reshape(seq_len, LANES)

    cos_a, sin = cos_sin(freq_a)
    sin_a = sin * jnp.where(lane < half, -1.0, 0.0)[None, :]
    sin_b = sin * jnp.where((lane >= half) & (lane < ROT_DIM), 1.0, 0.0)[None, :]
    cos_r, sin = cos_sin(freq_r)
    sign_r = jnp.where(jnp.arange(LANES) < RET_HEAD_DIM // 2, -1.0, 1.0)
    return jnp.concatenate([cos_a, sin_a, sin_b, cos_r, sin * sign_r[None, :]], axis=1)


def _layer(x, p, tables):
    b, s, _ = x.shape
    n = b * s
    h, un = _ffn(x.reshape(n, D_MODEL), p["g1"], p["wg1"], p["wu1"], p["wd1"], p["gm"])
    q, k, v, rq, rk, rkf, rv, gate, sb = _proj(un, s, p["win"], tables, p["ldf"], p["ldb"])
    seq = lambda a: a.reshape(b, s, a.shape[-1])
    sb = sb.reshape(b, s // RET_CHUNK, RET_HEAD_DIM, RET_WIDTH)
    attn, ret = _mix((p["sink"], seq(q), seq(k), seq(v), p["ga"]),
                     (p["ldf"], p["ldb"], seq(rq), seq(rk), seq(rkf), seq(rv), seq(gate), sb))
    y = _out_ffn(h, attn.reshape(n, ATTN_WIDTH), ret.reshape(n, RET_WIDTH),
                 p["woa"], p["wor"], p["g2"], p["wg2"], p["wu2"], p["wd2"], p["gf"])
    return y.reshape(b, s, D_MODEL)


def kernel(x_prompt, x_sample, ffn1_norm, ffn1_w_gate, ffn1_w_up, ffn1_w_down, mix_norm, w_in, attn_sink, attn_out_norm, ret_log_decay_fwd, ret_log_decay_bwd, w_out, ffn2_norm, ffn2_w_gate, ffn2_w_up, ffn2_w_down, final_norm):
    assert ffn1_norm.shape[0] == 1
    wo = w_out[0].astype(BF16)
    p = dict(
        g1=ffn1_norm[0][None, :], wg1=ffn1_w_gate[0].astype(BF16), wu1=ffn1_w_up[0].astype(BF16),
        wd1=ffn1_w_down[0].astype(BF16), gm=mix_norm[0][None, :],
        win=w_in[0].astype(BF16),
        sink=attn_sink[0], ga=attn_out_norm[0][None, :],
        ldf=ret_log_decay_fwd[0], ldb=ret_log_decay_bwd[0],
        woa=wo[:ATTN_WIDTH], wor=wo[ATTN_WIDTH:],
        g2=ffn2_norm[0][None, :], wg2=ffn2_w_gate[0].astype(BF16), wu2=ffn2_w_up[0].astype(BF16),
        wd2=ffn2_w_down[0].astype(BF16), gf=final_norm[None, :],
    )
    tables = _rope_tables(max(x_prompt.shape[1], x_sample.shape[1]))
    return (_layer(x_prompt, p, tables), _layer(x_sample, p, tables))
```
